```python
import math
import jax
import jax.numpy as jnp
from jax import lax
import numpy as np

D_MODEL = 1024
BATCH = 16
SEQ = 4096
DEPTH = 4

GRID_W = 64
CTX_LEN = 256
N_MIXERS = 4
QBLOCK = 128
ROPE_BASE = 10000.0
LN_EPS = 1e-5
RMS_EPS = 1e-6
NEG_INF = -1e30
DEEPNORM_ALPHA = (2.0 * DEPTH) ** 0.25
DEEPNORM_BETA = (8.0 * DEPTH) ** -0.25

MLA_HEADS = 16
MLA_NOPE_DIM = 64
MLA_ROPE_DIM = 32
MLA_V_DIM = 64
MLA_Q_LORA = 384
MLA_KV_LORA = 256

NA_HEADS = 16
NA_HEAD_DIM = 64
NA_ROWS = 8
NA_COLS = 16
NA_QCOLS = 16
NA_KCOLS = 32

DIFF_HEADS = 8
DIFF_HEAD_DIM = 64

SWA_HEADS = 16
SWA_KV_HEADS = 4
SWA_HEAD_DIM = 64
SWA_WINDOW = 128

FFN_HIDDEN = 2816
FFN_CONV = 3

kernel_name = 'hybrid_diffusion_prefix_trunk'


def layer_norm(x, g, b):
    xf = x.astype(jnp.float32)
    mu = jnp.mean(xf, -1, keepdims=True)
    var = jnp.mean(jnp.square(xf - mu), -1, keepdims=True)
    return ((xf - mu) * lax.rsqrt(var + LN_EPS)).astype(x.dtype) * g + b


def rms_norm(x, g):
    xf = x.astype(jnp.float32)
    return (xf * lax.rsqrt(jnp.mean(jnp.square(xf), -1, keepdims=True) + RMS_EPS)).astype(x.dtype) * g


def softmax_f32(s):
    return jax.nn.softmax(s.astype(jnp.float32), axis=-1)


def modulate(t, shift, scale):
    return t * (1.0 + scale) + shift


def axial_rope(n_tokens, rot_dim):
    t = jnp.arange(n_tokens)
    row = (t // GRID_W).astype(jnp.float32)
    col = (t % GRID_W).astype(jnp.float32)
    n_freq = rot_dim // 4
    inv_freq = ROPE_BASE ** (-jnp.arange(n_freq, dtype=jnp.float32) / n_freq)
    ang = jnp.concatenate([row[:, None] * inv_freq, col[:, None] * inv_freq], -1)
    return jnp.cos(ang), jnp.sin(ang)


def apply_rope(x, cos, sin):
    half = x.shape[-1] // 2
    shp = (cos.shape[0],) + (1,) * (x.ndim - 3) + (half,)
    c = cos.reshape(shp).astype(x.dtype)
    s = sin.reshape(shp).astype(x.dtype)
    x1, x2 = x[..., :half], x[..., half:]
    return jnp.concatenate([x1 * c - x2 * s, x2 * c + x1 * s], -1)


def ctx_attention(q, k, v):
    s = jnp.einsum('bqhd,bkhd->bhqk', q, k).astype(jnp.float32) * q.shape[-1] ** -0.5
    p = softmax_f32(s).astype(v.dtype)
    return jnp.einsum('bhqk,bkhd->bqhd', p, v)


def dense_attention(q_lat, q_ctx, k_lat, v_lat, k_ctx, v_ctx):
    B, S, H, dk = q_lat.shape
    dv = v_lat.shape[-1]
    n_ctx = k_ctx.shape[1]
    scale = dk ** -0.5
    nb = S // QBLOCK

    def to_blocks(t):
        return t.reshape(B, nb, QBLOCK, H, dk).swapaxes(0, 1)

    def one_block(qs):
        ql, qc = qs
        s = jnp.concatenate([jnp.einsum('bqhd,bkhd->bhqk', qc, k_ctx),
                             jnp.einsum('bqhd,bkhd->bhqk', ql, k_lat)], -1)
        p = softmax_f32(s.astype(jnp.float32) * scale).astype(v_lat.dtype)
        return (jnp.einsum('bhqk,bkhd->bqhd', p[..., :n_ctx], v_ctx)
                + jnp.einsum('bhqk,bkhd->bqhd', p[..., n_ctx:], v_lat))

    out = lax.map(one_block, (to_blocks(q_lat), to_blocks(q_ctx)))
    return out.swapaxes(0, 1).reshape(B, S, H, dv)


def mla_mixer(h, hc, w_in, q_norm, kv_norm, w_qb, w_kvb, w_o, need_ctx):
    H, DN, DR, DV, QL = MLA_HEADS, MLA_NOPE_DIM, MLA_ROPE_DIM, MLA_V_DIM, MLA_Q_LORA

    def project_q(t):
        cq = rms_norm(t @ w_in[:, :QL], q_norm)
        return (cq @ w_qb).reshape(t.shape[0], t.shape[1], H, DN + DR)

    def project_kv(t):
        a = t @ w_in[:, QL:]
        ckv = rms_norm(a[..., :MLA_KV_LORA], kv_norm)
        kv = (ckv @ w_kvb).reshape(t.shape[0], t.shape[1], H, DN + DV)
        return kv[..., :DN], kv[..., DN:], a[..., MLA_KV_LORA:]

    def full_key(k_nope, k_pe):
        return jnp.concatenate([k_nope, jnp.broadcast_to(k_pe[:, :, None, :], k_nope.shape[:3] + (DR,))], -1)

    B, S, _ = h.shape
    cos, sin = axial_rope(S, DR)
    q = project_q(h)
    k_nope, v, k_pe = project_kv(h)
    q_rot = jnp.concatenate([q[..., :DN], apply_rope(q[..., DN:], cos, sin)], -1)
    k = full_key(k_nope, apply_rope(k_pe, cos, sin))
    kc_nope, vc, kc_pe = project_kv(hc)
    kc = full_key(kc_nope, kc_pe)
    o = dense_attention(q_rot, q, k, v, kc, vc)
    y = o.reshape(B, S, H * DV) @ w_o
    yc = None
    if need_ctx:
        oc = ctx_attention(project_q(hc), kc, vc)
        yc = oc.reshape(oc.shape[0], oc.shape[1], H * DV) @ w_o
    return y, yc


def neighbourhood_attention(q, k, v, kc, vc, rpb):
    B, S, H, d = q.shape
    n_rows = S // GRID_W
    kr = min(NA_ROWS, n_rows)
    n_ctx = kc.shape[1]
    scale = d ** -0.5
    n_cb = GRID_W // NA_QCOLS
    q_cols = np.arange(GRID_W).reshape(n_cb, NA_QCOLS)
    win_start = np.clip(q_cols - NA_COLS // 2, 0, GRID_W - NA_COLS)
    band_start = np.clip(np.arange(n_cb) * NA_QCOLS - NA_COLS // 2, 0, GRID_W - NA_KCOLS)
    band_cols = band_start[:, None] + np.arange(NA_KCOLS)
    col_valid = ((band_cols[:, None, :] >= win_start[..., None])
                 & (band_cols[:, None, :] < win_start[..., None] + NA_COLS))
    col_idx = np.clip(band_cols[:, None, :] - q_cols[..., None] + NA_COLS - 1, 0, 2 * NA_COLS - 2)
    qg = q.reshape(B, n_rows, GRID_W, H, d)
    kg = k.reshape(B, n_rows, GRID_W, H, d)
    vg = v.reshape(B, n_rows, GRID_W, H, d)

    def row_block(r):
        r0 = jnp.clip(r - kr // 2, 0, n_rows - kr)
        kb = lax.dynamic_slice_in_dim(kg, r0, kr, axis=1)[:, :, band_cols]
        vb = lax.dynamic_slice_in_dim(vg, r0, kr, axis=1)[:, :, band_cols]
        qr = lax.dynamic_index_in_dim(qg, r, axis=1, keepdims=False).reshape(B, n_cb, NA_QCOLS, H, d)
        s = jnp.einsum('bjqhd,bmjnhd->bhjqmn', qr, kb).astype(jnp.float32) * scale
        row_idx = r0 + jnp.arange(kr) - r + NA_ROWS - 1
        bias = rpb[:, row_idx][:, :, col_idx].transpose(0, 2, 3, 1, 4)
        s = jnp.where(col_valid[:, :, None, :], s + bias.astype(jnp.float32), NEG_INF)
        s_ctx = jnp.einsum('bjqhd,bkhd->bhjqk', qr, kc).astype(jnp.float32) * scale
        p = softmax_f32(jnp.concatenate(
            [s_ctx, s.reshape(B, H, n_cb, NA_QCOLS, kr * NA_KCOLS)], -1)).astype(v.dtype)
        p_lat = p[..., n_ctx:].reshape(B, H, n_cb, NA_QCOLS, kr, NA_KCOLS)
        o = (jnp.einsum('bhjqk,bkhd->bjqhd', p[..., :n_ctx], vc)
             + jnp.einsum('bhjqmn,bmjnhd->bjqhd', p_lat, vb))
        return o.reshape(B, GRID_W, H, d)

    out = lax.map(row_block, jnp.arange(n_rows))
    return out.swapaxes(0, 1).reshape(B, S, H, d)


def na_mixer(h, hc, w_qkv, rpb, w_o, need_ctx):
    H, d = NA_HEADS, NA_HEAD_DIM
    nq = H * d

    def project_q(t):
        return (t @ w_qkv[:, :nq]).reshape(t.shape[0], t.shape[1], H, d)

    def project_kv(t):
        a = (t @ w_qkv[:, nq:]).reshape(t.shape[0], t.shape[1], 2, H, d)
        return a[:, :, 0], a[:, :, 1]

    B, S, _ = h.shape
    q = project_q(h)
    k, v = project_kv(h)
    kc, vc = project_kv(hc)
    y = neighbourhood_attention(q, k, v, kc, vc, rpb).reshape(B, S, H * d) @ w_o
    yc = None
    if need_ctx:
        oc = ctx_attention(project_q(hc), kc, vc)
        yc = oc.reshape(oc.shape[0], oc.shape[1], H * d) @ w_o
    return y, yc


def diff_mixer(h, hc, w_qkv, lam, subln, w_o, layer_idx, need_ctx):
    H, d = DIFF_HEADS, DIFF_HEAD_DIM
    nq = 2 * H * d

    def project_q(t):
        return (t @ w_qkv[:, :nq]).reshape(t.shape[0], t.shape[1], H, 2, d)

    def project_kv(t):
        a = t @ w_qkv[:, nq:]
        return (a[..., :nq].reshape(t.shape[0], t.shape[1], H, 2, d),
                a[..., nq:].reshape(t.shape[0], t.shape[1], H, 2 * d))

    lambda_init = 0.8 - 0.6 * math.exp(-0.3 * layer_idx)
    lf = lam.astype(jnp.float32)
    lam_full = jnp.exp(jnp.sum(lf[0] * lf[1])) - jnp.exp(jnp.sum(lf[2] * lf[3])) + lambda_init

    def combine(o1, o2):
        o = o1 - lam_full.astype(o1.dtype) * o2
        o = rms_norm(o, subln) * (1.0 - lambda_init)
        return o.reshape(o.shape[0], o.shape[1], H * 2 * d) @ w_o

    B, S, _ = h.shape
    cos, sin = axial_rope(S, d)
    q = project_q(h)
    k, v = project_kv(h)
    q_rot, k_rot = apply_rope(q, cos, sin), apply_rope(k, cos, sin)
    kc, vc = project_kv(hc)
    o1 = dense_attention(q_rot[:, :, :, 0], q[:, :, :, 0], k_rot[:, :, :, 0], v, kc[:, :, :, 0], vc)
    o2 = dense_attention(q_rot[:, :, :, 1], q[:, :, :, 1], k_rot[:, :, :, 1], v, kc[:, :, :, 1], vc)
    y = combine(o1, o2)
    yc = None
    if need_ctx:
        qc = project_q(hc)
        yc = combine(ctx_attention(qc[:, :, :, 0], kc[:, :, :, 0], vc),
                     ctx_attention(qc[:, :, :, 1], kc[:, :, :, 1], vc))
    return y, yc


def windowed_sink_attention(q_lat, q_ctx, k_lat, v_lat, k_ctx, v_ctx, sinks):
    B, S, G, R, d = q_lat.shape
    n_ctx = k_ctx.shape[1]
    scale = d ** -0.5
    nb = S // QBLOCK
    band = QBLOCK + 2 * SWA_WINDOW
    pad = ((0, 0), (SWA_WINDOW, SWA_WINDOW), (0, 0), (0, 0))
    kp = jnp.pad(k_lat, pad)
    vp = jnp.pad(v_lat, pad)
    sink_logit = sinks.astype(jnp.float32).reshape(G, R)[None, :, :, None, None]

    def one_block(n):
        start = n * QBLOCK
        ql = lax.dynamic_slice_in_dim(q_lat, start, QBLOCK, axis=1)
        qc = lax.dynamic_slice_in_dim(q_ctx, start, QBLOCK, axis=1)
        kb = lax.dynamic_slice_in_dim(kp, start, band, axis=1)
        vb = lax.dynamic_slice_in_dim(vp, start, band, axis=1)
        qpos = start + jnp.arange(QBLOCK)
        kpos = start - SWA_WINDOW + jnp.arange(band)
        valid = ((jnp.abs(qpos[:, None] - kpos[None, :]) <= SWA_WINDOW)
                 & (kpos[None, :] >= 0) & (kpos[None, :] < S))
        s_lat = jnp.where(valid, jnp.einsum('bqgrd,bkgd->bgrqk', ql, kb).astype(jnp.float32) * scale, NEG_INF)
        s_ctx = jnp.einsum('bqgrd,bkgd->bgrqk', qc, k_ctx).astype(jnp.float32) * scale
        sink = jnp.broadcast_to(sink_logit, s_ctx.shape[:-1] + (1,))
        p = softmax_f32(jnp.concatenate([sink, s_ctx, s_lat], -1))[..., 1:].astype(v_lat.dtype)
        return (jnp.einsum('bgrqk,bkgd->bqgrd', p[..., :n_ctx], v_ctx)
                + jnp.einsum('bgrqk,bkgd->bqgrd', p[..., n_ctx:], vb))

    out = lax.map(one_block, jnp.arange(nb))
    return out.swapaxes(0, 1).reshape(B, S, G * R * d)


def ctx_sink_attention(q, k, v, sinks):
    B, L, G, R, d = q.shape
    s = jnp.einsum('bqgrd,bkgd->bgrqk', q, k).astype(jnp.float32) * d ** -0.5
    sink = jnp.broadcast_to(sinks.astype(jnp.float32).reshape(G, R)[None, :, :, None, None], s.shape[:-1] + (1,))
    p = softmax_f32(jnp.concatenate([sink, s], -1))[..., 1:].astype(v.dtype)
    return jnp.einsum('bgrqk,bkgd->bqgrd', p, v).reshape(B, L, G * R * d)


def swa_mixer(h, hc, w_qkv, sinks, w_o, need_ctx):
    H, G, d = SWA_HEADS, SWA_KV_HEADS, SWA_HEAD_DIM
    R = H // G
    nq = H * d

    def project_q(t):
        return (t @ w_qkv[:, :nq]).reshape(t.shape[0], t.shape[1], G, R, d)

    def project_kv(t):
        a = t @ w_qkv[:, nq:]
        return (a[..., :G * d].reshape(t.shape[0], t.shape[1], G, d),
                a[..., G * d:].reshape(t.shape[0], t.shape[1], G, d))

    B, S, _ = h.shape
    cos, sin = axial_rope(S, d)
    q = project_q(h)
    k, v = project_kv(h)
    kc, vc = project_kv(hc)
    o = windowed_sink_attention(apply_rope(q, cos, sin), q, apply_rope(k, cos, sin), v, kc, vc, sinks)
    y = o @ w_o
    yc = None
    if need_ctx:
        yc = ctx_sink_attention(project_q(hc), kc, vc, sinks) @ w_o
    return y, yc


def conv_ffn(h, w_in, conv_w, conv_b, w_out):
    u = h @ w_in
    u = lax.conv_general_dilated(
        u, conv_w[:, None, :].astype(u.dtype), window_strides=(1,),
        padding=[(FFN_CONV // 2, FFN_CONV // 2)],
        dimension_numbers=('NWC', 'WIO', 'NWC'),
        feature_group_count=u.shape[-1]) + conv_b
    a, g = jnp.split(u, 2, axis=-1)
    return (jax.nn.silu(g) * a) @ w_out


def setup_inputs(seed: int = 0) -> dict:
    key = jax.random.key(seed)
    ks = iter(jax.random.split(key, 40))

    def nrm(shape, scale):
        return jax.random.normal(next(ks), shape, jnp.float32) * scale

    D, F = D_MODEL, FFN_HIDDEN
    beta = DEEPNORM_BETA
    n_a, n_b, n_c, n_d = [len(range(m, DEPTH, N_MIXERS)) for m in range(N_MIXERS)]
    mla_in = MLA_Q_LORA + MLA_KV_LORA + MLA_ROPE_DIM
    return {
        'x': nrm((BATCH, SEQ, D), 1.0),
        'c': nrm((BATCH, D), 1.0),
        'ctx': nrm((BATCH, CTX_LEN, D), 1.0),
        'c_ctx': nrm((D,), 1.0),
        'ada_w': nrm((DEPTH, D, 6 * D), 0.5 * D ** -0.5),
        'ada_b': nrm((DEPTH, 6 * D), 0.01),
        'ln1_g': 1.0 + nrm((DEPTH, D), 0.02),
        'ln1_b': nrm((DEPTH, D), 0.02),
        'ln2_g': 1.0 + nrm((DEPTH, D), 0.02),
        'ln2_b': nrm((DEPTH, D), 0.02),
        'ffn_w_in': nrm((DEPTH, D, 2 * F), D ** -0.5),
        'ffn_conv_w': nrm((DEPTH, FFN_CONV, 2 * F), FFN_CONV ** -0.5),
        'ffn_conv_b': nrm((DEPTH, 2 * F), 0.01),
        'ffn_w_out': nrm((DEPTH, F, D), beta * F ** -0.5),
        'mla_w_in': nrm((n_a, D, mla_in), D ** -0.5),
        'mla_q_norm': 1.0 + nrm((n_a, MLA_Q_LORA), 0.02),
        'mla_kv_norm': 1.0 + nrm((n_a, MLA_KV_LORA), 0.02),
        'mla_w_qb': nrm((n_a, MLA_Q_LORA, MLA_HEADS * (MLA_NOPE_DIM + MLA_ROPE_DIM)), MLA_Q_LORA ** -0.5),
        'mla_w_kvb': nrm((n_a, MLA_KV_LORA, MLA_HEADS * (MLA_NOPE_DIM + MLA_V_DIM)), MLA_KV_LORA ** -0.5),
        'mla_w_o': nrm((n_a, MLA_HEADS * MLA_V_DIM, D), beta * (MLA_HEADS * MLA_V_DIM) ** -0.5),
        'na_w_qkv': nrm((n_b, D, 3 * NA_HEADS * NA_HEAD_DIM), D ** -0.5),
        'na_rpb': nrm((n_b, NA_HEADS, 2 * NA_ROWS - 1, 2 * NA_COLS - 1), 0.1),
        'na_w_o': nrm((n_b, NA_HEADS * NA_HEAD_DIM, D), beta * (NA_HEADS * NA_HEAD_DIM) ** -0.5),
        'diff_w_qkv': nrm((n_c, D, 6 * DIFF_HEADS * DIFF_HEAD_DIM), D ** -0.5),
        'diff_lambda': nrm((n_c, 4, DIFF_HEAD_DIM), 0.1),
        'diff_subln': 1.0 + nrm((n_c, 2 * DIFF_HEAD_DIM), 0.02),
        'diff_w_o': nrm((n_c, DIFF_HEADS * 2 * DIFF_HEAD_DIM, D), beta * (DIFF_HEADS * 2 * DIFF_HEAD_DIM) ** -0.5),
        'swa_w_qkv': nrm((n_d, D, (SWA_HEADS + 2 * SWA_KV_HEADS) * SWA_HEAD_DIM), D ** -0.5),
        'swa_sinks': nrm((n_d, SWA_HEADS), 0.5),
        'swa_w_o': nrm((n_d, SWA_HEADS * SWA_HEAD_DIM, D), beta * (SWA_HEADS * SWA_HEAD_DIM) ** -0.5),
    }


def reference(x, c, ctx, c_ctx, ada_w, ada_b, ln1_g, ln1_b, ln2_g, ln2_b,
              ffn_w_in, ffn_conv_w, ffn_conv_b, ffn_w_out,
              mla_w_in, mla_q_norm, mla_kv_norm, mla_w_qb, mla_w_kvb, mla_w_o,
              na_w_qkv, na_rpb, na_w_o,
              diff_w_qkv, diff_lambda, diff_subln, diff_w_o,
              swa_w_qkv, swa_sinks, swa_w_o):
    silu_c = jax.nn.silu(c)
    silu_cc = jax.nn.silu(c_ctx)
    h_lat, h_ctx = x, ctx
    for i in range(DEPTH):
        last = i == DEPTH - 1
        need_ctx = not last
        m = (silu_c @ ada_w[i] + ada_b[i])[:, None, :]
        sh1, sc1, g1, sh2, sc2, g2 = jnp.split(m, 6, axis=-1)
        mc = silu_cc @ ada_w[i] + ada_b[i]
        csh1, csc1, cg1, csh2, csc2, cg2 = jnp.split(mc, 6, axis=-1)
        a_lat = modulate(h_lat, sh1, sc1)
        a_ctx = modulate(h_ctx, csh1, csc1)
        kind, slot = i % N_MIXERS, i // N_MIXERS
        if kind == 0:
            y, yc = mla_mixer(a_lat, a_ctx, mla_w_in[slot], mla_q_norm[slot], mla_kv_norm[slot],
                              mla_w_qb[slot], mla_w_kvb[slot], mla_w_o[slot], need_ctx)
        elif kind == 1:
            y, yc = na_mixer(a_lat, a_ctx, na_w_qkv[slot], na_rpb[slot], na_w_o[slot], need_ctx)
        elif kind == 2:
            y, yc = diff_mixer(a_lat, a_ctx, diff_w_qkv[slot], diff_lambda[slot], diff_subln[slot],
                               diff_w_o[slot], i, need_ctx)
        else:
            y, yc = swa_mixer(a_lat, a_ctx, swa_w_qkv[slot], swa_sinks[slot], swa_w_o[slot], need_ctx)
        h_lat = layer_norm(DEEPNORM_ALPHA * h_lat + g1 * y, ln1_g[i], ln1_b[i])
        f = conv_ffn(modulate(h_lat, sh2, sc2), ffn_w_in[i], ffn_conv_w[i], ffn_conv_b[i], ffn_w_out[i])
        h_lat = layer_norm(DEEPNORM_ALPHA * h_lat + g2 * f, ln2_g[i], ln2_b[i])
        if need_ctx:
            h_ctx = layer_norm(DEEPNORM_ALPHA * h_ctx + cg1 * yc, ln1_g[i], ln1_b[i])
            fc = conv_ffn(modulate(h_ctx, csh2, csc2), ffn_w_in[i], ffn_conv_w[i], ffn_conv_b[i], ffn_w_out[i])
            h_ctx = layer_norm(DEEPNORM_ALPHA * h_ctx + cg2 * fc, ln2_g[i], ln2_b[i])
    return h_lat
```

```python
import functools
import math

import numpy as np
import jax
import jax.numpy as jnp
from jax import lax
from jax.experimental import pallas as pl
from jax.experimental.pallas import tpu as pltpu

F32 = jnp.float32
BF16 = jnp.bfloat16

GRID_W = 64
ROPE_BASE = 10000.0
LN_EPS = 1e-5
RMS_EPS = 1e-6
NEG_INF = -1e30
LOG2E = 1.4426950408889634

MLA_HEADS, MLA_NOPE, MLA_ROPE, MLA_V, MLA_QL, MLA_KVL = 16, 64, 32, 64, 384, 256
NA_HEADS, NA_DIM, NA_ROWS, NA_COLS = 16, 64, 8, 16
DIFF_HEADS, DIFF_DIM = 8, 64
SWA_HEADS, SWA_KV, SWA_DIM, SWA_WINDOW = 16, 4, 64, 128
FFN_CHUNK = 256

LANES = 128
TILE = 256
NA_TQ = 128
NA_KROWS = 10
DENSE_TK = 512
MOD_ROWS = 24
VMEM_LIMIT = 56 * 1024 * 1024


def _cparams(n_axes):
    return pltpu.CompilerParams(dimension_semantics=("arbitrary",) * n_axes, vmem_limit_bytes=VMEM_LIMIT)


def _const_spec(shape):
    n = len(shape)
    return pl.BlockSpec(shape, lambda *_: (0,) * n, pipeline_mode=pl.Buffered(1))


def _silu(x):
    return x * (1.0 / (1.0 + jnp.exp(-x)))


def _layer_norm(x, g, b):
    mu = jnp.mean(x, axis=-1, keepdims=True)
    xc = x - mu
    var = jnp.mean(xc * xc, axis=-1, keepdims=True)
    return xc * lax.rsqrt(var + LN_EPS) * g + b


def _rms_norm(x, g):
    return x * lax.rsqrt(jnp.mean(x * x, axis=-1, keepdims=True) + RMS_EPS) * g


def _qk(q, k):
    return lax.dot_general(q, k, (((1,), (1,)), ((), ())), preferred_element_type=F32)


def _rope(slab, cos, sin):
    return slab * cos + pltpu.roll(slab, 64, 1) * sin


def _adaln_kernel(c_ref, w_ref, b_ref, o_ref):
    sc = _silu(c_ref[...])
    o_ref[0] = jnp.dot(sc, w_ref[0], preferred_element_type=F32, precision=lax.Precision.HIGHEST) + b_ref[0]


def _adaln(cc, ada_w, ada_b):
    depth, d, n = ada_w.shape
    tn = 1024
    out = pl.pallas_call(
        _adaln_kernel,
        grid=(depth, n // tn),
        in_specs=[pl.BlockSpec((MOD_ROWS, d), lambda l, j: (0, 0)),
                  pl.BlockSpec((1, d, tn), lambda l, j: (l, 0, j)),
                  pl.BlockSpec((1, 1, tn), lambda l, j: (l, 0, j))],
        out_specs=pl.BlockSpec((1, MOD_ROWS, tn), lambda l, j: (l, 0, j)),
        out_shape=jax.ShapeDtypeStruct((depth, MOD_ROWS, n), F32),
        compiler_params=_cparams(2),
        name="adaln",
    )(cc, ada_w, ada_b.reshape(depth, 1, n))
    return out.reshape(depth, MOD_ROWS, 6, 1, d)


def _qkv_kernel(*refs, plan, use_rope):
    if use_rope:
        mod_ref, x_ref, w_ref, cos_ref, sin_ref = refs[:5]
        out_refs = refs[5:]
        cos, sin = cos_ref[...], sin_ref[...]
    else:
        mod_ref, x_ref, w_ref = refs[:3]
        out_refs = refs[3:]
    sh, sc = mod_ref[0, 0], mod_ref[0, 1]
    a = (x_ref[0] * (1.0 + sc) + sh).astype(BF16)
    y = jnp.dot(a, w_ref[...], preferred_element_type=F32)
    for o_ref, (c0, width, rope) in zip(out_refs, plan):
        if rope:
            for j in range(width // LANES):
                slab = y[:, c0 + LANES * j:c0 + LANES * (j + 1)]
                o_ref[0, :, LANES * j:LANES * (j + 1)] = _rope(slab, cos, sin).astype(BF16)
        else:
            o_ref[0] = y[:, c0:c0 + width].astype(BF16)


def _qkv_proj(h, mods, w, plan, cos=None, sin=None, *, name):
    b, tt, d = h.shape
    n = w.shape[1]
    use_rope = cos is not None
    nt = tt // TILE
    in_specs = [pl.BlockSpec((1, 6, 1, d), lambda i, t: (jnp.where(t == 0, b, i), 0, 0, 0)),
                pl.BlockSpec((1, TILE, d), lambda i, t: (i, t, 0)),
                _const_spec((d, n))]
    args = [mods, h, w]
    if use_rope:
        in_specs += [pl.BlockSpec((TILE, LANES), lambda i, t: (t, 0))] * 2
        args += [cos, sin]
    return pl.pallas_call(
        functools.partial(_qkv_kernel, plan=plan, use_rope=use_rope),
        grid=(b, nt),
        in_specs=in_specs,
        out_specs=[pl.BlockSpec((1, TILE, wd), lambda i, t: (i, t, 0)) for _, wd, _ in plan],
        out_shape=[jax.ShapeDtypeStruct((b, tt, wd), BF16) for _, wd, _ in plan],
        compiler_params=_cparams(2),
        name=name,
    )(*args)


def _mla_proj_kernel(mod_ref, x_ref, w1_ref, qn_ref, kvn_ref, wqb_ref, wkvb_ref, cos_ref, sin_ref,
                     qr_ref, qu_ref, k_ref, v_ref):
    sh, sc = mod_ref[0, 0], mod_ref[0, 1]
    cos, sin = cos_ref[...], sin_ref[...]
    a = (x_ref[0] * (1.0 + sc) + sh).astype(BF16)
    y1 = jnp.dot(a, w1_ref[...], preferred_element_type=F32)
    cq = _rms_norm(y1[:, :MLA_QL], qn_ref[...]).astype(BF16)
    ckv = _rms_norm(y1[:, MLA_QL:MLA_QL + MLA_KVL], kvn_ref[...]).astype(BF16)
    pe = _rope(y1[:, MLA_QL + MLA_KVL:], cos, sin)
    q = jnp.dot(cq, wqb_ref[...], preferred_element_type=F32)
    kv = jnp.dot(ckv, wkvb_ref[...], preferred_element_type=F32)
    for hd in range(MLA_HEADS):
        sl = slice(LANES * hd, LANES * (hd + 1))
        qs = q[:, sl]
        qu_ref[0, :, sl] = qs.astype(BF16)
        qr_ref[0, :, sl] = _rope(qs, cos, sin).astype(BF16)
        k_ref[0, :, sl] = (kv[:, sl] + pe).astype(BF16)
    v_ref[0] = kv[:, MLA_HEADS * LANES:].astype(BF16)


def _mla_proj(h, mods, w1, qn, kvn, wqb, wkvb, cos, sin):
    b, tt, d = h.shape
    nt = tt // TILE
    hw = MLA_HEADS * LANES
    vw = MLA_HEADS * MLA_V
    row = lambda i, t: (i, t, 0)
    return pl.pallas_call(
        _mla_proj_kernel,
        grid=(b, nt),
        in_specs=[pl.BlockSpec((1, 6, 1, d), lambda i, t: (jnp.where(t == 0, b, i), 0, 0, 0)),
                  pl.BlockSpec((1, TILE, d), row),
                  _const_spec(w1.shape), _const_spec(qn.shape), _const_spec(kvn.shape),
                  _const_spec(wqb.shape), _const_spec(wkvb.shape),
                  pl.BlockSpec((TILE, LANES), lambda i, t: (t, 0)),
                  pl.BlockSpec((TILE, LANES), lambda i, t: (t, 0))],
        out_specs=[pl.BlockSpec((1, TILE, hw), row), pl.BlockSpec((1, TILE, hw), row),
                   pl.BlockSpec((1, TILE, hw), row), pl.BlockSpec((1, TILE, vw), row)],
        out_shape=[jax.ShapeDtypeStruct((b, tt, hw), BF16)] * 3 + [jax.ShapeDtypeStruct((b, tt, vw), BF16)],
        compiler_params=_cparams(2),
        name="mla_proj",
    )(mods, h, w1, qn, kvn, wqb, wkvb, cos, sin)


def _dense_attn_kernel(*refs, kind, ctx_len, n_lat, lam_init):
    if kind == "mla":
        qr_ref, qu_ref, k_ref, v_ref, o_ref, m_sc, l_sc, acc_sc = refs
    else:
        qr_ref, qu_ref, k_ref, v_ref, lam_ref, sub_ref, o_ref, m_sc, l_sc, acc_sc = refs
    t = pl.program_id(2)
    lane = lax.broadcasted_iota(jnp.int32, (1, LANES), 1)
    vc = v_ref[0, 0:ctx_len, :]
    for s in range(2):
        if kind == "mla":
            ksl = slice(LANES * s, LANES * (s + 1))
            qu, qr = qu_ref[0, :, ksl], qr_ref[0, :, ksl]
        else:
            ksl = slice(0, LANES)
            keep = ((lane // 32) % 2) == s
            qu = jnp.where(keep, qu_ref[0], jnp.zeros_like(qu_ref[0]))
            qr = jnp.where(keep, qr_ref[0], jnp.zeros_like(qr_ref[0]))
        sc = _qk(qu, k_ref[0, 0:ctx_len, ksl])
        m = jnp.max(sc, axis=1, keepdims=True)
        p = jnp.exp2(sc - m)
        m_sc[s] = m
        l_sc[s] = jnp.sum(p, axis=1, keepdims=True)
        acc_sc[s] = jnp.dot(p.astype(BF16), vc, preferred_element_type=F32)

        @pl.when(t > 0)
        def _latent():
            def body(i, carry):
                m0, l0, acc0 = carry
                start = pl.multiple_of(ctx_len + i * DENSE_TK, DENSE_TK // 2)
                s_l = _qk(qr, k_ref[0, pl.ds(start, DENSE_TK), ksl])
                m1 = jnp.maximum(m0, jnp.max(s_l, axis=1, keepdims=True))
                alpha = jnp.exp2(m0 - m1)
                p_l = jnp.exp2(s_l - m1)
                l1 = alpha * l0 + jnp.sum(p_l, axis=1, keepdims=True)
                acc1 = alpha * acc0 + jnp.dot(p_l.astype(BF16), v_ref[0, pl.ds(start, DENSE_TK), :],
                                              preferred_element_type=F32)
                return m1, l1, acc1

            m2, l2, acc2 = lax.fori_loop(0, n_lat, body, (m_sc[s], l_sc[s], acc_sc[s]))
            m_sc[s] = m2
            l_sc[s] = l2
            acc_sc[s] = acc2

    o0 = acc_sc[0] / l_sc[0]
    o1 = acc_sc[1] / l_sc[1]
    if kind == "mla":
        o_ref[0] = jnp.where(lane < 64, o0, o1).astype(BF16)
    else:
        lf = lam_ref[...]
        lam = (jnp.exp(jnp.sum(lf[0:1] * lf[1:2], axis=1, keepdims=True))
               - jnp.exp(jnp.sum(lf[2:3] * lf[3:4], axis=1, keepdims=True)) + lam_init)
        o = o0 - lam * o1
        o_ref[0] = (_rms_norm(o, sub_ref[...]) * (1.0 - lam_init)).astype(BF16)


def _dense_attn(qr, qu, k, v, *, kind, ctx_len, lam=None, subln=None, lam_init=0.0):
    b, tt, _ = v.shape
    groups = v.shape[2] // LANES
    qw = qr.shape[2] // groups
    nt = tt // TILE
    n_lat = (tt - ctx_len) // DENSE_TK
    in_specs = [pl.BlockSpec((1, TILE, qw), lambda i, g, t: (i, t, g)),
                pl.BlockSpec((1, TILE, qw), lambda i, g, t: (i, t, g)),
                pl.BlockSpec((1, tt, qw), lambda i, g, t: (i, 0, g)),
                pl.BlockSpec((1, tt, LANES), lambda i, g, t: (i, 0, g))]
    args = [qr, qu, k, v]
    if kind == "diff":
        in_specs += [_const_spec(lam.shape), _const_spec(subln.shape)]
        args += [lam, subln]
    return pl.pallas_call(
        functools.partial(_dense_attn_kernel, kind=kind, ctx_len=ctx_len, n_lat=n_lat, lam_init=lam_init),
        grid=(b, groups, nt),
        in_specs=in_specs,
        out_specs=pl.BlockSpec((1, TILE, LANES), lambda i, g, t: (i, t, g)),
        out_shape=jax.ShapeDtypeStruct((b, tt, groups * LANES), BF16),
        scratch_shapes=[pltpu.VMEM((2, TILE, 1), F32), pltpu.VMEM((2, TILE, 1), F32),
                        pltpu.VMEM((2, TILE, LANES), F32)],
        compiler_params=_cparams(3),
        name=kind + "_attn",
    )(*args)


def _softmax_pv(segments, sink=None):
    m = None
    for s, _ in segments:
        ms = jnp.max(s, axis=1, keepdims=True)
        m = ms if m is None else jnp.maximum(m, ms)
    if sink is not None:
        m = jnp.maximum(m, sink)
    l = jnp.exp2(sink - m) if sink is not None else 0.0
    acc = None
    for s, v in segments:
        p = jnp.exp2(s - m)
        l = l + jnp.sum(p, axis=1, keepdims=True)
        pv = jnp.dot(p.astype(BF16), v, preferred_element_type=F32)
        acc = pv if acc is None else acc + pv
    return acc / l


def _na_attn_kernel(q_ref, k_ref, v_ref, bias_ref, o_ref, *, ctx_len, n_rows):
    t = pl.program_id(2)
    n_ctx_tiles = ctx_len // NA_TQ
    lane = lax.broadcasted_iota(jnp.int32, (1, LANES), 1)
    q = q_ref[0]
    kc, vc = k_ref[0, 0:ctx_len, :], v_ref[0, 0:ctx_len, :]
    qs = [jnp.where(lane < 64, q, jnp.zeros_like(q)), jnp.where(lane >= 64, q, jnp.zeros_like(q))]

    @pl.when(t < n_ctx_tiles)
    def _ctx():
        o = [_softmax_pv([(_qk(qs[s], kc), vc)]) for s in range(2)]
        o_ref[0] = jnp.where(lane < 64, o[0], o[1]).astype(BF16)

    @pl.when(t >= n_ctx_tiles)
    def _lat():
        j = t - n_ctx_tiles
        row0 = jnp.clip(2 * j - NA_ROWS // 2, 0, n_rows - NA_KROWS)
        start = pl.multiple_of(ctx_len + row0 * GRID_W, GRID_W)
        kb = k_ref[0, pl.ds(start, NA_KROWS * GRID_W), :]
        vb = v_ref[0, pl.ds(start, NA_KROWS * GRID_W), :]
        o = [_softmax_pv([(_qk(qs[s], kc), vc), (_qk(qs[s], kb) + bias_ref[0, s], vb)]) for s in range(2)]
        o_ref[0] = jnp.where(lane < 64, o[0], o[1]).astype(BF16)


def _na_tile_class(j, n_tiles):
    return jnp.where(j < 2, j, jnp.where(j >= n_tiles - 2, j - (n_tiles - 2) + 3, 2))


def _na_attn(q, k, v, bias, *, ctx_len):
    b, tt, w = q.shape
    groups = w // LANES
    n_rows = (tt - ctx_len) // GRID_W
    n_ctx_tiles = ctx_len // NA_TQ
    n_lat_tiles = n_rows // 2
    nk = NA_KROWS * GRID_W

    def bias_map(i, g, t):
        return (_na_tile_class(jnp.maximum(t - n_ctx_tiles, 0), n_lat_tiles), g, 0, 0)

    return pl.pallas_call(
        functools.partial(_na_attn_kernel, ctx_len=ctx_len, n_rows=n_rows),
        grid=(b, groups, tt // NA_TQ),
        in_specs=[pl.BlockSpec((1, NA_TQ, LANES), lambda i, g, t: (i, t, g)),
                  pl.BlockSpec((1, tt, LANES), lambda i, g, t: (i, 0, g)),
                  pl.BlockSpec((1, tt, LANES), lambda i, g, t: (i, 0, g)),
                  pl.BlockSpec((1, 2, NA_TQ, nk), bias_map)],
        out_specs=pl.BlockSpec((1, NA_TQ, LANES), lambda i, g, t: (i, t, g)),
        out_shape=jax.ShapeDtypeStruct((b, tt, w), BF16),
        compiler_params=_cparams(3),
        name="na_attn",
    )(q, k, v, bias)


def _na_bias_table(rpb, n_rows):
    n_tiles = n_rows // 2
    reps = np.array([0, 1, 2, n_tiles - 2, n_tiles - 1])
    qi = np.arange(NA_TQ)
    kn = np.arange(NA_KROWS * GRID_W)
    r = 2 * reps[:, None] + qi[None, :] // GRID_W
    c = np.broadcast_to(qi % GRID_W, r.shape)
    row0 = np.clip(2 * reps - NA_ROWS // 2, 0, n_rows - NA_KROWS)
    kr = row0[:, None] + kn[None, :] // GRID_W
    kc = np.broadcast_to(kn % GRID_W, kr.shape)
    kr_win = min(NA_ROWS, n_rows)
    r0 = np.clip(r - kr_win // 2, 0, n_rows - kr_win)
    cs = np.clip(c - NA_COLS // 2, 0, GRID_W - NA_COLS)
    valid = ((kr[:, None, :] >= r0[:, :, None]) & (kr[:, None, :] < r0[:, :, None] + kr_win)
             & (kc[:, None, :] >= cs[:, :, None]) & (kc[:, None, :] < cs[:, :, None] + NA_COLS))
    ridx = np.clip(kr[:, None, :] - r[:, :, None] + NA_ROWS - 1, 0, 2 * NA_ROWS - 2)
    cidx = np.clip(kc[:, None, :] - c[:, :, None] + NA_COLS - 1, 0, 2 * NA_COLS - 2)
    flat = jnp.asarray(ridx * (2 * NA_COLS - 1) + cidx)
    tab = (rpb.astype(F32) * LOG2E).reshape(rpb.shape[0], -1)
    gathered = jnp.take(tab, flat, axis=1)
    out = jnp.where(jnp.asarray(valid)[None], gathered, NEG_INF)
    return out.transpose(1, 0, 2, 3)


def _swa_attn_kernel(qr_ref, qu_ref, k_ref, v_ref, sink_ref, o_ref, *, ctx_len, seq):
    t = pl.program_id(2)
    band = TILE + 2 * SWA_WINDOW
    lane = lax.broadcasted_iota(jnp.int32, (1, LANES), 1)
    kc, vc = k_ref[0, 0:ctx_len, :], v_ref[0, 0:ctx_len, :]
    n_slabs = qr_ref.shape[2] // LANES

    def heads(q_ref):
        out = []
        for i in range(n_slabs):
            slab = q_ref[0, :, LANES * i:LANES * (i + 1)]
            for s in range(2):
                out.append(jnp.where(((lane // 32) % 2) == s, slab, jnp.zeros_like(slab)))
        return out

    def sink(idx):
        return sink_ref[0, idx:idx + 1, 0:1]

    def store(outs):
        for i in range(n_slabs):
            o_ref[0, :, LANES * i:LANES * (i + 1)] = jnp.where(lane < 64, outs[2 * i], outs[2 * i + 1]).astype(BF16)

    @pl.when(t == 0)
    def _ctx():
        store([_softmax_pv([(_qk(q, kc), vc)], sink(n)) for n, q in enumerate(heads(qu_ref))])

    @pl.when(t > 0)
    def _lat():
        p0 = (t - 1) * TILE
        rel = jnp.clip(p0 - SWA_WINDOW, 0, seq - band)
        start = pl.multiple_of(ctx_len + rel, SWA_WINDOW)
        kb = k_ref[0, pl.ds(start, band), :]
        vb = v_ref[0, pl.ds(start, band), :]
        qpos = p0 + lax.broadcasted_iota(jnp.int32, (TILE, band), 0)
        kpos = rel + lax.broadcasted_iota(jnp.int32, (TILE, band), 1)
        valid = jnp.abs(qpos - kpos) <= SWA_WINDOW
        outs = []
        for n, (qu, qr) in enumerate(zip(heads(qu_ref), heads(qr_ref))):
            s_l = jnp.where(valid, _qk(qr, kb), NEG_INF)
            outs.append(_softmax_pv([(_qk(qu, kc), vc), (s_l, vb)], sink(n)))
        store(outs)


def _swa_attn(qr, qu, k, v, sinks, *, ctx_len):
    b, tt, w = qr.shape
    groups = k.shape[2] // LANES
    qw = w // groups
    return pl.pallas_call(
        functools.partial(_swa_attn_kernel, ctx_len=ctx_len, seq=tt - ctx_len),
        grid=(b, groups, tt // TILE),
        in_specs=[pl.BlockSpec((1, TILE, qw), lambda i, g, t: (i, t, g)),
                  pl.BlockSpec((1, TILE, qw), lambda i, g, t: (i, t, g)),
                  pl.BlockSpec((1, tt, LANES), lambda i, g, t: (i, 0, g)),
                  pl.BlockSpec((1, tt, LANES), lambda i, g, t: (i, 0, g)),
                  pl.BlockSpec((1, sinks.shape[1], LANES), lambda i, g, t: (g, 0, 0))],
        out_specs=pl.BlockSpec((1, TILE, qw), lambda i, g, t: (i, t, g)),
        out_shape=jax.ShapeDtypeStruct((b, tt, w), BF16),
        compiler_params=_cparams(3),
        name="swa_attn",
    )(qr, qu, k, v, sinks)


def _out_kernel(mod_ref, o_ref, w_ref, h_ref, g_ref, b_ref, out_ref, *, alpha):
    y = jnp.dot(o_ref[0], w_ref[...], preferred_element_type=F32)
    out_ref[0] = _layer_norm(alpha * h_ref[0] + mod_ref[0, 2] * y, g_ref[...], b_ref[...])


def _out_proj(o, w_o, h, mods, g, bb, *, alpha):
    b, tt, d = h.shape
    row = lambda i, t: (i, t, 0)
    return pl.pallas_call(
        functools.partial(_out_kernel, alpha=alpha),
        grid=(b, tt // TILE),
        in_specs=[pl.BlockSpec((1, 6, 1, d), lambda i, t: (jnp.where(t == 0, b, i), 0, 0, 0)),
                  pl.BlockSpec((1, TILE, o.shape[2]), row),
                  _const_spec(w_o.shape),
                  pl.BlockSpec((1, TILE, d), row),
                  _const_spec(g.shape), _const_spec(bb.shape)],
        out_specs=pl.BlockSpec((1, TILE, d), row),
        out_shape=jax.ShapeDtypeStruct((b, tt, d), F32),
        compiler_params=_cparams(2),
        name="out_proj_ln",
    )(mods, o, w_o, h, g, bb)


def _ffn_kernel(mod_ref, h_ref, hp_ref, hn_ref, win_ref, cw_ref, cb_ref, wout_ref, g_ref, b_ref, out_ref,
                *, alpha, n_chunks, first_lat, last):
    t = pl.program_id(1)
    sh, sc, gate = mod_ref[0, 3], mod_ref[0, 4], mod_ref[0, 5]
    h = h_ref[0]
    has_prev = jnp.where((t == 0) | (t == first_lat), 0.0, 1.0)
    has_next = jnp.where((t == first_lat - 1) | (t == last), 0.0, 1.0)
    mod = lambda x: x * (1.0 + sc) + sh
    xe = jnp.concatenate([mod(hp_ref[0]) * has_prev, mod(h), mod(hn_ref[0]) * has_next], axis=0).astype(BF16)
    rows = h.shape[0]

    def chunk(c, acc):
        u = jnp.dot(xe, win_ref[c], preferred_element_type=F32)
        cw = cw_ref[c]
        conv = (cw[0:1] * pltpu.roll(u, 1, 0)[8:8 + rows] + cw[1:2] * u[8:8 + rows]
                + cw[2:3] * pltpu.roll(u, rows + 15, 0)[8:8 + rows] + cb_ref[c])
        act = (_silu(conv[:, FFN_CHUNK:]) * conv[:, :FFN_CHUNK]).astype(BF16)
        return acc + jnp.dot(act, wout_ref[c], preferred_element_type=F32)

    f = lax.fori_loop(0, n_chunks, chunk, jnp.zeros(h.shape, F32))
    out_ref[0] = _layer_norm(alpha * h + gate * f, g_ref[...], b_ref[...])


def _conv_ffn(h, mods, win, cw, cb, wout, g, bb, *, alpha, ctx_len):
    b, tt, d = h.shape
    nt = tt // TILE
    halo = 8
    per = TILE // halo
    row = lambda i, t: (i, t, 0)
    return pl.pallas_call(
        functools.partial(_ffn_kernel, alpha=alpha, n_chunks=win.shape[0], first_lat=ctx_len // TILE, last=nt - 1),
        grid=(b, nt),
        in_specs=[pl.BlockSpec((1, 6, 1, d), lambda i, t: (jnp.where(t == 0, b, i), 0, 0, 0)),
                  pl.BlockSpec((1, TILE, d), row),
                  pl.BlockSpec((1, halo, d), lambda i, t: (i, jnp.maximum(t * per - 1, 0), 0)),
                  pl.BlockSpec((1, halo, d), lambda i, t: (i, jnp.minimum((t + 1) * per, tt // halo - 1), 0)),
                  _const_spec(win.shape), _const_spec(cw.shape), _const_spec(cb.shape), _const_spec(wout.shape),
                  _const_spec(g.shape), _const_spec(bb.shape)],
        out_specs=pl.BlockSpec((1, TILE, d), row),
        out_shape=jax.ShapeDtypeStruct((b, tt, d), F32),
        compiler_params=_cparams(2),
        name="conv_ffn_ln",
    )(mods, h, h, h, win, cw, cb, wout, g, bb)


def _pair_perm():
    return np.concatenate([np.arange(0, 32), np.arange(64, 96), np.arange(32, 64), np.arange(96, 128)])


def _take_cols(w, src):
    src = np.asarray(src)
    cols = jnp.take(w, jnp.asarray(np.maximum(src, 0)), axis=1)
    return jnp.where(jnp.asarray(src >= 0)[None, :], cols, 0.0)


def _axial_rope(n_tokens, rot_dim):
    t = jnp.arange(n_tokens)
    row = (t // GRID_W).astype(F32)
    col = (t % GRID_W).astype(F32)
    n_freq = rot_dim // 4
    inv_freq = ROPE_BASE ** (-jnp.arange(n_freq, dtype=F32) / n_freq)
    ang = jnp.concatenate([row[:, None] * inv_freq, col[:, None] * inv_freq], -1)
    return jnp.cos(ang), jnp.sin(ang)


def _rope_tables(seq, ctx_len, kind):
    if kind == "mla":
        cos, sin = _axial_rope(seq, MLA_ROPE)
        c = jnp.ones((seq, LANES), F32).at[:, 48:64].set(cos).at[:, 112:128].set(cos)
        s = jnp.zeros((seq, LANES), F32).at[:, 48:64].set(-sin).at[:, 112:128].set(sin)
    else:
        cos, sin = _axial_rope(seq, 64)
        c = jnp.concatenate([cos] * 4, axis=1)
        s = jnp.concatenate([-sin, -sin, sin, sin], axis=1)
    c = jnp.concatenate([jnp.ones((ctx_len, LANES), F32), c], axis=0)
    s = jnp.concatenate([jnp.zeros((ctx_len, LANES), F32), s], axis=0)
    return c, s


def _mla_head_lanes():
    nope = np.concatenate([np.arange(0, 48), np.arange(64, 80)])
    rope_a, rope_b = np.arange(48, 64), np.arange(112, 128)
    return nope, rope_a, rope_b


def _mla_weights(w_in, w_qb, w_kvb):
    nope, rope_a, rope_b = _mla_head_lanes()
    dq = MLA_NOPE + MLA_ROPE
    src_pe = np.full(LANES, -1)
    src_pe[rope_a] = MLA_QL + MLA_KVL + np.arange(16)
    src_pe[rope_b] = MLA_QL + MLA_KVL + 16 + np.arange(16)
    w1 = jnp.concatenate([w_in[:, :MLA_QL + MLA_KVL], _take_cols(w_in, src_pe)], axis=1)
    src_q = np.full(MLA_HEADS * LANES, -1)
    src_k = np.full(MLA_HEADS * LANES, -1)
    src_v = np.zeros(MLA_HEADS * MLA_V, np.int64)
    for hd in range(MLA_HEADS):
        src_q[hd * LANES + nope] = hd * dq + np.arange(MLA_NOPE)
        src_q[hd * LANES + rope_a] = hd * dq + MLA_NOPE + np.arange(16)
        src_q[hd * LANES + rope_b] = hd * dq + MLA_NOPE + 16 + np.arange(16)
        src_k[hd * LANES + nope] = hd * (MLA_NOPE + MLA_V) + np.arange(MLA_NOPE)
        src_v[hd * MLA_V:(hd + 1) * MLA_V] = hd * (MLA_NOPE + MLA_V) + MLA_NOPE + np.arange(MLA_V)
    wqb = _take_cols(w_qb, src_q) * (dq ** -0.5 * LOG2E)
    wkvb = jnp.concatenate([_take_cols(w_kvb, src_k), _take_cols(w_kvb, src_v)], axis=1)
    return w1.astype(BF16), wqb.astype(BF16), wkvb.astype(BF16)


def _swa_layout():
    rep = SWA_HEADS // SWA_KV
    head_a = [2 * (j // rep) * rep + j % rep for j in range(SWA_HEADS // 2)]
    head_b = [(2 * (j // rep) + 1) * rep + j % rep for j in range(SWA_HEADS // 2)]
    return head_a, head_b


def kernel(x, c, ctx, c_ctx, ada_w, ada_b, ln1_g, ln1_b, ln2_g, ln2_b, ffn_w_in, ffn_conv_w, ffn_conv_b, ffn_w_out, mla_w_in, mla_q_norm, mla_kv_norm, mla_w_qb, mla_w_kvb, mla_w_o, na_w_qkv, na_rpb, na_w_o, diff_w_qkv, diff_lambda, diff_subln, diff_w_o, swa_w_qkv, swa_sinks, swa_w_o):
    b, seq, d = x.shape
    ctx_len = ctx.shape[1]
    depth = ada_w.shape[0]
    assert ctx_len == TILE and seq % DENSE_TK == 0 and seq % (2 * GRID_W) == 0 and b + 1 <= MOD_ROWS
    alpha = (2.0 * depth) ** 0.25
    n_rows = seq // GRID_W
    perm = _pair_perm()

    cc = jnp.concatenate([c, c_ctx[None, :], jnp.zeros((MOD_ROWS - b - 1, d), F32)], axis=0)
    mods_all = _adaln(cc, ada_w, ada_b)
    h = jnp.concatenate([ctx, x], axis=1)

    f_hidden = ffn_w_out.shape[1]
    n_chunks = f_hidden // FFN_CHUNK

    for i in range(depth):
        mods = mods_all[i]
        kind, slot = i % 4, i // 4
        if kind == 0:
            w1, wqb, wkvb = _mla_weights(mla_w_in[slot], mla_w_qb[slot], mla_w_kvb[slot])
            cos, sin = _rope_tables(seq, ctx_len, "mla")
            qr, qu, k, v = _mla_proj(h, mods, w1, mla_q_norm[slot][None, :], mla_kv_norm[slot][None, :],
                                     wqb, wkvb, cos, sin)
            o = _dense_attn(qr, qu, k, v, kind="mla", ctx_len=ctx_len)
            w_o = mla_w_o[slot]
        elif kind == 1:
            nq = NA_HEADS * NA_DIM
            w = na_w_qkv[slot]
            w = jnp.concatenate([w[:, :nq] * (NA_DIM ** -0.5 * LOG2E), w[:, nq:]], axis=1).astype(BF16)
            q, k, v = _qkv_proj(h, mods, w, ((0, nq, False), (nq, nq, False), (2 * nq, nq, False)), name="na_proj")
            o = _na_attn(q, k, v, _na_bias_table(na_rpb[slot], n_rows), ctx_len=ctx_len)
            w_o = na_w_o[slot]
        elif kind == 2:
            nq = 2 * DIFF_HEADS * DIFF_DIM
            src = np.concatenate([g * LANES + perm for g in range(2 * nq // LANES)] + [np.arange(2 * nq, 3 * nq)])
            w = _take_cols(diff_w_qkv[slot], src)
            w = jnp.concatenate([w[:, :nq] * (DIFF_DIM ** -0.5 * LOG2E), w[:, nq:]], axis=1).astype(BF16)
            cos, sin = _rope_tables(seq, ctx_len, "pair")
            qr, qu, k, v = _qkv_proj(h, mods, w, ((0, nq, True), (0, nq, False), (nq, nq, True), (2 * nq, nq, False)),
                                     cos, sin, name="diff_proj")
            lam_init = 0.8 - 0.6 * math.exp(-0.3 * i)
            o = _dense_attn(qr, qu, k, v, kind="diff", ctx_len=ctx_len, lam=diff_lambda[slot],
                            subln=diff_subln[slot][None, :], lam_init=lam_init)
            w_o = diff_w_o[slot]
        else:
            nq, nkv = SWA_HEADS * SWA_DIM, SWA_KV * SWA_DIM
            head_a, head_b = _swa_layout()
            pair_src = lambda ha, hb, base: base + np.concatenate([ha * 64 + np.arange(64), hb * 64 + np.arange(64)])[perm]
            src = np.concatenate([pair_src(ha, hb, 0) for ha, hb in zip(head_a, head_b)]
                                 + [pair_src(2 * g, 2 * g + 1, nq) for g in range(SWA_KV // 2)]
                                 + [np.arange(nq + nkv, nq + 2 * nkv)])
            w = _take_cols(swa_w_qkv[slot], src)
            w = jnp.concatenate([w[:, :nq] * (SWA_DIM ** -0.5 * LOG2E), w[:, nq:]], axis=1).astype(BF16)
            cos, sin = _rope_tables(seq, ctx_len, "pair")
            qr, qu, k, v = _qkv_proj(h, mods, w, ((0, nq, True), (0, nq, False), (nq, nkv, True), (nq + nkv, nkv, False)),
                                     cos, sin, name="swa_proj")
            order = np.array([hd for pair in zip(head_a, head_b) for hd in pair])
            per_group = len(order) // (SWA_KV // 2)
            sinks = jnp.broadcast_to((swa_sinks[slot].astype(F32) * LOG2E)[order].reshape(-1, per_group, 1),
                                     (SWA_KV // 2, per_group, LANES))
            o = _swa_attn(qr, qu, k, v, sinks, ctx_len=ctx_len)
            rows = np.concatenate([hd * 64 + np.arange(64) for hd in order])
            w_o = jnp.take(swa_w_o[slot], jnp.asarray(rows), axis=0)
        h = _out_proj(o, w_o.astype(BF16), h, mods, ln1_g[i][None, :], ln1_b[i][None, :], alpha=alpha)

        win = ffn_w_in[i].reshape(d, 2, n_chunks, FFN_CHUNK).transpose(2, 0, 1, 3).reshape(n_chunks, d, 2 * FFN_CHUNK)
        cw = ffn_conv_w[i].reshape(3, 2, n_chunks, FFN_CHUNK).transpose(2, 0, 1, 3).reshape(n_chunks, 3, 2 * FFN_CHUNK)
        cb = ffn_conv_b[i].reshape(2, n_chunks, FFN_CHUNK).transpose(1, 0, 2).reshape(n_chunks, 1, 2 * FFN_CHUNK)
        wout = ffn_w_out[i].reshape(n_chunks, FFN_CHUNK, d)
        h = _conv_ffn(h, mods, win.astype(BF16), cw, cb, wout.astype(BF16), ln2_g[i][None, :], ln2_b[i][None, :],
                      alpha=alpha, ctx_len=ctx_len)
    return h[:, ctx_len:, :]
```

```python
import functools
import math

import numpy as np
import jax
import jax.numpy as jnp
from jax import lax
from jax.experimental import pallas as pl
from jax.experimental.pallas import tpu as pltpu

F32 = jnp.float32
BF16 = jnp.bfloat16

GRID_W = 64
ROPE_BASE = 10000.0
LN_EPS = 1e-5
RMS_EPS = 1e-6
NEG_INF = -1e30
LOG2E = 1.4426950408889634

MLA_HEADS, MLA_NOPE, MLA_ROPE, MLA_V, MLA_QL, MLA_KVL = 16, 64, 32, 64, 384, 256
NA_HEADS, NA_DIM, NA_ROWS, NA_COLS = 16, 64, 8, 16
DIFF_HEADS, DIFF_DIM = 8, 64
SWA_HEADS, SWA_KV, SWA_DIM, SWA_WINDOW = 16, 4, 64, 128
FFN_CHUNK = 256

LANES = 128
TILE = 256
NA_TQ = 128
NA_KROWS = 10
NA_GROUPS = 4
DENSE_TK = 512
MOD_ROWS = 24
VMEM_LIMIT = 56 * 1024 * 1024


def _cparams(n_axes):
    return pltpu.CompilerParams(dimension_semantics=("arbitrary",) * n_axes, vmem_limit_bytes=VMEM_LIMIT)


def _const_spec(shape):
    n = len(shape)
    return pl.BlockSpec(shape, lambda *_: (0,) * n, pipeline_mode=pl.Buffered(1))


def _silu(x):
    return x * (1.0 / (1.0 + jnp.exp(-x)))


def _layer_norm(x, g, b):
    mu = jnp.mean(x, axis=-1, keepdims=True)
    xc = x - mu
    var = jnp.mean(xc * xc, axis=-1, keepdims=True)
    return xc * lax.rsqrt(var + LN_EPS) * g + b


def _rms_norm(x, g):
    return x * lax.rsqrt(jnp.mean(x * x, axis=-1, keepdims=True) + RMS_EPS) * g


def _qk(q, k):
    return lax.dot_general(q, k, (((1,), (1,)), ((), ())), preferred_element_type=F32)


def _rope(slab, cos, sin):
    return slab * cos + pltpu.roll(slab, 64, 1) * sin


def _adaln_kernel(c_ref, w_ref, b_ref, o_ref):
    sc = _silu(c_ref[...])
    o_ref[0] = jnp.dot(sc, w_ref[0], preferred_element_type=F32, precision=lax.Precision.HIGHEST) + b_ref[0]


def _adaln(cc, ada_w, ada_b):
    depth, d, n = ada_w.shape
    tn = 1024
    out = pl.pallas_call(
        _adaln_kernel,
        grid=(depth, n // tn),
        in_specs=[pl.BlockSpec((MOD_ROWS, d), lambda l, j: (0, 0)),
                  pl.BlockSpec((1, d, tn), lambda l, j: (l, 0, j)),
                  pl.BlockSpec((1, 1, tn), lambda l, j: (l, 0, j))],
        out_specs=pl.BlockSpec((1, MOD_ROWS, tn), lambda l, j: (l, 0, j)),
        out_shape=jax.ShapeDtypeStruct((depth, MOD_ROWS, n), F32),
        compiler_params=_cparams(2),
        name="adaln",
    )(cc, ada_w, ada_b.reshape(depth, 1, n))
    return out.reshape(depth, MOD_ROWS, 6, 1, d)


def _qkv_kernel(*refs, plan, use_rope):
    if use_rope:
        mod_ref, x_ref, w_ref, cos_ref, sin_ref = refs[:5]
        out_refs = refs[5:]
        cos, sin = cos_ref[...], sin_ref[...]
    else:
        mod_ref, x_ref, w_ref = refs[:3]
        out_refs = refs[3:]
    sh, sc = mod_ref[0, 0], mod_ref[0, 1]
    a = (x_ref[0] * (1.0 + sc) + sh).astype(BF16)
    y = jnp.dot(a, w_ref[...], preferred_element_type=F32)
    for o_ref, (c0, width, rope) in zip(out_refs, plan):
        if rope:
            for j in range(width // LANES):
                slab = y[:, c0 + LANES * j:c0 + LANES * (j + 1)]
                o_ref[0, :, LANES * j:LANES * (j + 1)] = _rope(slab, cos, sin).astype(BF16)
        else:
            o_ref[0] = y[:, c0:c0 + width].astype(BF16)


def _qkv_proj(h, mods, w, plan, cos=None, sin=None, *, name):
    b, tt, d = h.shape
    n = w.shape[1]
    use_rope = cos is not None
    nt = tt // TILE
    in_specs = [pl.BlockSpec((1, 6, 1, d), lambda i, t: (jnp.where(t == 0, b, i), 0, 0, 0)),
                pl.BlockSpec((1, TILE, d), lambda i, t: (i, t, 0)),
                _const_spec((d, n))]
    args = [mods, h, w]
    if use_rope:
        in_specs += [pl.BlockSpec((TILE, LANES), lambda i, t: (t, 0))] * 2
        args += [cos, sin]
    return pl.pallas_call(
        functools.partial(_qkv_kernel, plan=plan, use_rope=use_rope),
        grid=(b, nt),
        in_specs=in_specs,
        out_specs=[pl.BlockSpec((1, TILE, wd), lambda i, t: (i, t, 0)) for _, wd, _ in plan],
        out_shape=[jax.ShapeDtypeStruct((b, tt, wd), BF16) for _, wd, _ in plan],
        compiler_params=_cparams(2),
        name=name,
    )(*args)


def _mla_proj_kernel(mod_ref, x_ref, w1_ref, qn_ref, kvn_ref, wqb_ref, wkvb_ref, cos_ref, sin_ref,
                     qr_ref, qu_ref, k_ref, v_ref):
    sh, sc = mod_ref[0, 0], mod_ref[0, 1]
    cos, sin = cos_ref[...], sin_ref[...]
    a = (x_ref[0] * (1.0 + sc) + sh).astype(BF16)
    y1 = jnp.dot(a, w1_ref[...], preferred_element_type=F32)
    cq = _rms_norm(y1[:, :MLA_QL], qn_ref[...]).astype(BF16)
    ckv = _rms_norm(y1[:, MLA_QL:MLA_QL + MLA_KVL], kvn_ref[...]).astype(BF16)
    pe = _rope(y1[:, MLA_QL + MLA_KVL:], cos, sin)
    q = jnp.dot(cq, wqb_ref[...], preferred_element_type=F32)
    kv = jnp.dot(ckv, wkvb_ref[...], preferred_element_type=F32)
    for hd in range(MLA_HEADS):
        sl = slice(LANES * hd, LANES * (hd + 1))
        qs = q[:, sl]
        qu_ref[0, :, sl] = qs.astype(BF16)
        qr_ref[0, :, sl] = _rope(qs, cos, sin).astype(BF16)
        k_ref[0, :, sl] = (kv[:, sl] + pe).astype(BF16)
    v_ref[0] = kv[:, MLA_HEADS * LANES:].astype(BF16)


def _mla_proj(h, mods, w1, qn, kvn, wqb, wkvb, cos, sin):
    b, tt, d = h.shape
    nt = tt // TILE
    hw = MLA_HEADS * LANES
    vw = MLA_HEADS * MLA_V
    row = lambda i, t: (i, t, 0)
    return pl.pallas_call(
        _mla_proj_kernel,
        grid=(b, nt),
        in_specs=[pl.BlockSpec((1, 6, 1, d), lambda i, t: (jnp.where(t == 0, b, i), 0, 0, 0)),
                  pl.BlockSpec((1, TILE, d), row),
                  _const_spec(w1.shape), _const_spec(qn.shape), _const_spec(kvn.shape),
                  _const_spec(wqb.shape), _const_spec(wkvb.shape),
                  pl.BlockSpec((TILE, LANES), lambda i, t: (t, 0)),
                  pl.BlockSpec((TILE, LANES), lambda i, t: (t, 0))],
        out_specs=[pl.BlockSpec((1, TILE, hw), row), pl.BlockSpec((1, TILE, hw), row),
                   pl.BlockSpec((1, TILE, hw), row), pl.BlockSpec((1, TILE, vw), row)],
        out_shape=[jax.ShapeDtypeStruct((b, tt, hw), BF16)] * 3 + [jax.ShapeDtypeStruct((b, tt, vw), BF16)],
        compiler_params=_cparams(2),
        name="mla_proj",
    )(mods, h, w1, qn, kvn, wqb, wkvb, cos, sin)


def _dense_attn_kernel(*refs, kind, ctx_len, n_lat, lam_init):
    if kind == "mla":
        qr_ref, qu_ref, k_ref, vt_ref, o_ref, m_sc, l_sc, acc_sc = refs
    else:
        qr_ref, qu_ref, k_ref, vt_ref, lam_ref, sub_ref, o_ref, m_sc, l_sc, acc_sc = refs
    t = pl.program_id(2)
    lane = lax.broadcasted_iota(jnp.int32, (1, LANES), 1)
    sub_tiles = DENSE_TK // TILE

    def queries(q_ref, s):
        if kind == "mla":
            return q_ref[0, :, LANES * s:LANES * (s + 1)]
        keep = ((lane // 32) % 2) == s
        return jnp.where(keep, q_ref[0], jnp.zeros_like(q_ref[0]))

    def keys(start, size, s):
        return k_ref[0, pl.ds(start, size), LANES * s:LANES * (s + 1)] if kind == "mla" else k_ref[0, pl.ds(start, size), :]

    vt_c = vt_ref[0, 0, 0]
    for s in range(2):
        st = _qk(keys(0, ctx_len, s), queries(qu_ref, s))
        m = jnp.max(st, axis=0, keepdims=True)
        p = jnp.exp2(st - m)
        m_sc[s] = m
        l_sc[s] = jnp.sum(p, axis=0, keepdims=True)
        acc_sc[s] = jnp.dot(vt_c, p.astype(BF16), preferred_element_type=F32)

    @pl.when(t > 0)
    def _latent():
        q_rot = [queries(qr_ref, s) for s in range(2)]

        def body(i, carry):
            start = pl.multiple_of(ctx_len + i * DENSE_TK, TILE)
            out = []
            for s in range(2):
                m0, l0, acc0 = carry[s]
                st = _qk(keys(start, DENSE_TK, s), q_rot[s])
                m1 = jnp.maximum(m0, jnp.max(st, axis=0, keepdims=True))
                alpha = jnp.exp2(m0 - m1)
                p = jnp.exp2(st - m1)
                l1 = alpha * l0 + jnp.sum(p, axis=0, keepdims=True)
                pb = p.astype(BF16)
                pv = None
                for j in range(sub_tiles):
                    d = jnp.dot(vt_ref[0, 0, 1 + i * sub_tiles + j], pb[TILE * j:TILE * (j + 1)],
                                preferred_element_type=F32)
                    pv = d if pv is None else pv + d
                out.append((m1, l1, alpha * acc0 + pv))
            return tuple(out)

        res = lax.fori_loop(0, n_lat, body, tuple((m_sc[s], l_sc[s], acc_sc[s]) for s in range(2)), unroll=2)
        for s in range(2):
            m_sc[s], l_sc[s], acc_sc[s] = res[s]

    o0 = acc_sc[0] / l_sc[0]
    o1 = acc_sc[1] / l_sc[1]
    if kind == "mla":
        o = jnp.concatenate([o0[:64], o1[64:]], axis=0)
    else:
        lf = lam_ref[...]
        lam = (jnp.exp(jnp.sum(lf[0:1] * lf[1:2], axis=1, keepdims=True))
               - jnp.exp(jnp.sum(lf[2:3] * lf[3:4], axis=1, keepdims=True)) + lam_init)
        o = o0 - lam * o1
        o = o * lax.rsqrt(jnp.mean(o * o, axis=0, keepdims=True) + RMS_EPS) * sub_ref[...] * (1.0 - lam_init)
    o_ref[0] = o.T.astype(BF16)


def _dense_attn(qr, qu, k, v, *, kind, ctx_len, lam=None, subln=None, lam_init=0.0):
    b, tt, _ = v.shape
    groups = v.shape[2] // LANES
    qw = qr.shape[2] // groups
    nt = tt // TILE
    n_lat = (tt - ctx_len) // DENSE_TK
    vt = v.reshape(b, nt, TILE, groups, LANES).transpose(0, 3, 1, 4, 2)
    in_specs = [pl.BlockSpec((1, TILE, qw), lambda i, g, t: (i, t, g)),
                pl.BlockSpec((1, TILE, qw), lambda i, g, t: (i, t, g)),
                pl.BlockSpec((1, tt, qw), lambda i, g, t: (i, 0, g)),
                pl.BlockSpec((1, 1, nt, LANES, TILE), lambda i, g, t: (i, g, 0, 0, 0))]
    args = [qr, qu, k, vt]
    if kind == "diff":
        in_specs += [_const_spec(lam.shape), _const_spec(subln.shape)]
        args += [lam, subln]
    return pl.pallas_call(
        functools.partial(_dense_attn_kernel, kind=kind, ctx_len=ctx_len, n_lat=n_lat, lam_init=lam_init),
        grid=(b, groups, nt),
        in_specs=in_specs,
        out_specs=pl.BlockSpec((1, TILE, LANES), lambda i, g, t: (i, t, g)),
        out_shape=jax.ShapeDtypeStruct((b, tt, groups * LANES), BF16),
        scratch_shapes=[pltpu.VMEM((2, 1, TILE), F32), pltpu.VMEM((2, 1, TILE), F32),
                        pltpu.VMEM((2, LANES, TILE), F32)],
        compiler_params=_cparams(3),
        name=kind + "_attn",
    )(*args)


def _softmax_pv(segments, sink=None):
    m = None
    for s, _ in segments:
        ms = jnp.max(s, axis=1, keepdims=True)
        m = ms if m is None else jnp.maximum(m, ms)
    if sink is not None:
        m = jnp.maximum(m, sink)
    l = jnp.exp2(sink - m) if sink is not None else 0.0
    acc = None
    for s, v in segments:
        p = jnp.exp2(s - m)
        l = l + jnp.sum(p, axis=1, keepdims=True)
        pv = jnp.dot(p.astype(BF16), v, preferred_element_type=F32)
        acc = pv if acc is None else acc + pv
    return acc / l


def _na_attn_kernel(q_ref, k_ref, v_ref, bias_ref, o_ref, *, ctx_len, n_rows):
    t = pl.program_id(2)
    n_ctx_tiles = ctx_len // NA_TQ
    lane = lax.broadcasted_iota(jnp.int32, (1, LANES), 1)
    nk = NA_KROWS * GRID_W

    def head_pair(i):
        sl = slice(LANES * i, LANES * (i + 1))
        q = q_ref[0, :, sl]
        return sl, [jnp.where(lane < 64, q, jnp.zeros_like(q)), jnp.where(lane >= 64, q, jnp.zeros_like(q))]

    @pl.when(t < n_ctx_tiles)
    def _ctx():
        for i in range(q_ref.shape[2] // LANES):
            sl, qs = head_pair(i)
            kc, vc = k_ref[0, 0:ctx_len, sl], v_ref[0, 0:ctx_len, sl]
            o = [_softmax_pv([(_qk(qs[s], kc), vc)]) for s in range(2)]
            o_ref[0, :, sl] = jnp.where(lane < 64, o[0], o[1]).astype(BF16)

    @pl.when(t >= n_ctx_tiles)
    def _lat():
        j = t - n_ctx_tiles
        row0 = jnp.clip(2 * j - NA_ROWS // 2, 0, n_rows - NA_KROWS)
        start = pl.multiple_of(ctx_len + row0 * GRID_W, GRID_W)
        for i in range(q_ref.shape[2] // LANES):
            sl, qs = head_pair(i)
            kc, vc = k_ref[0, 0:ctx_len, sl], v_ref[0, 0:ctx_len, sl]
            kb, vb = k_ref[0, pl.ds(start, nk), sl], v_ref[0, pl.ds(start, nk), sl]
            o = [_softmax_pv([(_qk(qs[s], kc), vc), (_qk(qs[s], kb) + bias_ref[0, 2 * i + s], vb)])
                 for s in range(2)]
            o_ref[0, :, sl] = jnp.where(lane < 64, o[0], o[1]).astype(BF16)


def _na_tile_class(j, n_tiles):
    return jnp.where(j < 2, j, jnp.where(j >= n_tiles - 2, j - (n_tiles - 2) + 3, 2))


def _na_attn(q, k, v, bias, *, ctx_len):
    b, tt, w = q.shape
    groups = w // LANES
    n_rows = (tt - ctx_len) // GRID_W
    n_ctx_tiles = ctx_len // NA_TQ
    n_lat_tiles = n_rows // 2
    nk = NA_KROWS * GRID_W
    gw = NA_GROUPS * LANES

    def bias_map(i, g, t):
        return (_na_tile_class(jnp.maximum(t - n_ctx_tiles, 0), n_lat_tiles), g, 0, 0)

    return pl.pallas_call(
        functools.partial(_na_attn_kernel, ctx_len=ctx_len, n_rows=n_rows),
        grid=(b, groups // NA_GROUPS, tt // NA_TQ),
        in_specs=[pl.BlockSpec((1, NA_TQ, gw), lambda i, g, t: (i, t, g)),
                  pl.BlockSpec((1, tt, gw), lambda i, g, t: (i, 0, g)),
                  pl.BlockSpec((1, tt, gw), lambda i, g, t: (i, 0, g)),
                  pl.BlockSpec((1, 2 * NA_GROUPS, NA_TQ, nk), bias_map)],
        out_specs=pl.BlockSpec((1, NA_TQ, gw), lambda i, g, t: (i, t, g)),
        out_shape=jax.ShapeDtypeStruct((b, tt, w), BF16),
        compiler_params=_cparams(3),
        name="na_attn",
    )(q, k, v, bias)


def _na_bias_table(rpb, n_rows):
    heads, n_ri, n_ci = rpb.shape
    n_tiles = n_rows // 2
    w = jnp.pad(rpb.astype(F32) * LOG2E, ((0, 0), (0, 0), (0, LANES - n_ci)))
    r = jnp.broadcast_to(w[:, :, None, :], (heads, n_ri, GRID_W, LANES)).reshape(heads, n_ri, GRID_W * LANES)
    r = r[:, :, :GRID_W * (LANES - 1)].reshape(heads, n_ri, GRID_W, LANES - 1)
    toep = r[..., NA_COLS - 1:NA_COLS - 1 + GRID_W]
    c = np.arange(GRID_W)
    cs = np.clip(c - NA_COLS // 2, 0, GRID_W - NA_COLS)
    col_ok = (c[None, :] >= cs[:, None]) & (c[None, :] < cs[:, None] + NA_COLS)
    toep = jnp.where(jnp.asarray(col_ok), toep, NEG_INF)
    masked = jnp.full((heads, GRID_W, GRID_W), NEG_INF, F32)
    classes = []
    for rep in (0, 1, 2, n_tiles - 2, n_tiles - 1):
        row0 = int(np.clip(2 * rep - NA_ROWS // 2, 0, n_rows - NA_KROWS))
        q_rows = []
        for a in range(2):
            qr = 2 * rep + a
            r0 = int(np.clip(qr - NA_ROWS // 2, 0, n_rows - NA_ROWS))
            blocks = [toep[:, kr - qr + NA_ROWS - 1] if r0 <= kr < r0 + NA_ROWS else masked
                      for kr in range(row0, row0 + NA_KROWS)]
            q_rows.append(jnp.concatenate(blocks, axis=-1))
        classes.append(jnp.concatenate(q_rows, axis=1))
    return jnp.stack(classes)


def _swa_attn_kernel(qr_ref, qu_ref, k_ref, v_ref, sink_ref, o_ref, *, ctx_len, seq):
    t = pl.program_id(2)
    band = TILE + 2 * SWA_WINDOW
    lane = lax.broadcasted_iota(jnp.int32, (1, LANES), 1)
    kc, vc = k_ref[0, 0:ctx_len, :], v_ref[0, 0:ctx_len, :]
    n_slabs = qr_ref.shape[2] // LANES

    def heads(q_ref):
        out = []
        for i in range(n_slabs):
            slab = q_ref[0, :, LANES * i:LANES * (i + 1)]
            for s in range(2):
                out.append(jnp.where(((lane // 32) % 2) == s, slab, jnp.zeros_like(slab)))
        return out

    def sink(idx):
        return sink_ref[0, idx:idx + 1, 0:1]

    def store(outs):
        for i in range(n_slabs):
            o_ref[0, :, LANES * i:LANES * (i + 1)] = jnp.where(lane < 64, outs[2 * i], outs[2 * i + 1]).astype(BF16)

    @pl.when(t == 0)
    def _ctx():
        store([_softmax_pv([(_qk(q, kc), vc)], sink(n)) for n, q in enumerate(heads(qu_ref))])

    @pl.when(t > 0)
    def _lat():
        p0 = (t - 1) * TILE
        rel = jnp.clip(p0 - SWA_WINDOW, 0, seq - band)
        start = pl.multiple_of(ctx_len + rel, SWA_WINDOW)
        kb = k_ref[0, pl.ds(start, band), :]
        vb = v_ref[0, pl.ds(start, band), :]
        qpos = p0 + lax.broadcasted_iota(jnp.int32, (TILE, band), 0)
        kpos = rel + lax.broadcasted_iota(jnp.int32, (TILE, band), 1)
        valid = jnp.abs(qpos - kpos) <= SWA_WINDOW
        outs = []
        for n, (qu, qr) in enumerate(zip(heads(qu_ref), heads(qr_ref))):
            s_l = jnp.where(valid, _qk(qr, kb), NEG_INF)
            outs.append(_softmax_pv([(_qk(qu, kc), vc), (s_l, vb)], sink(n)))
        store(outs)


def _swa_attn(qr, qu, k, v, sinks, *, ctx_len):
    b, tt, w = qr.shape
    groups = k.shape[2] // LANES
    qw = w // groups
    return pl.pallas_call(
        functools.partial(_swa_attn_kernel, ctx_len=ctx_len, seq=tt - ctx_len),
        grid=(b, groups, tt // TILE),
        in_specs=[pl.BlockSpec((1, TILE, qw), lambda i, g, t: (i, t, g)),
                  pl.BlockSpec((1, TILE, qw), lambda i, g, t: (i, t, g)),
                  pl.BlockSpec((1, tt, LANES), lambda i, g, t: (i, 0, g)),
                  pl.BlockSpec((1, tt, LANES), lambda i, g, t: (i, 0, g)),
                  pl.BlockSpec((1, sinks.shape[1], LANES), lambda i, g, t: (g, 0, 0))],
        out_specs=pl.BlockSpec((1, TILE, qw), lambda i, g, t: (i, t, g)),
        out_shape=jax.ShapeDtypeStruct((b, tt, w), BF16),
        compiler_params=_cparams(3),
        name="swa_attn",
    )(qr, qu, k, v, sinks)


def _out_kernel(mod_ref, o_ref, w_ref, h_ref, g_ref, b_ref, out_ref, *, alpha):
    y = jnp.dot(o_ref[0], w_ref[...], preferred_element_type=F32)
    out_ref[0] = _layer_norm(alpha * h_ref[0] + mod_ref[0, 2] * y, g_ref[...], b_ref[...])


def _out_proj(o, w_o, h, mods, g, bb, *, alpha):
    b, tt, d = h.shape
    row = lambda i, t: (i, t, 0)
    return pl.pallas_call(
        functools.partial(_out_kernel, alpha=alpha),
        grid=(b, tt // TILE),
        in_specs=[pl.BlockSpec((1, 6, 1, d), lambda i, t: (jnp.where(t == 0, b, i), 0, 0, 0)),
                  pl.BlockSpec((1, TILE, o.shape[2]), row),
                  _const_spec(w_o.shape),
                  pl.BlockSpec((1, TILE, d), row),
                  _const_spec(g.shape), _const_spec(bb.shape)],
        out_specs=pl.BlockSpec((1, TILE, d), row),
        out_shape=jax.ShapeDtypeStruct((b, tt, d), F32),
        compiler_params=_cparams(2),
        name="out_proj_ln",
    )(mods, o, w_o, h, g, bb)


def _ffn_kernel(mod_ref, h_ref, hp_ref, hn_ref, win_ref, cw_ref, cb_ref, wout_ref, g_ref, b_ref, out_ref,
                *, alpha, n_chunks, first_lat, last):
    t = pl.program_id(1)
    sh, sc, gate = mod_ref[0, 3], mod_ref[0, 4], mod_ref[0, 5]
    h = h_ref[0]
    has_prev = jnp.where((t == 0) | (t == first_lat), 0.0, 1.0)
    has_next = jnp.where((t == first_lat - 1) | (t == last), 0.0, 1.0)
    mod = lambda x: x * (1.0 + sc) + sh
    xe = jnp.concatenate([mod(hp_ref[0]) * has_prev, mod(h), mod(hn_ref[0]) * has_next], axis=0).astype(BF16)
    rows = h.shape[0]

    def up(c):
        return jnp.dot(xe, win_ref[c], preferred_element_type=F32)

    def down(c, u, acc):
        cw = cw_ref[c]
        conv = (cw[0:1] * pltpu.roll(u, 1, 0)[8:8 + rows] + cw[1:2] * u[8:8 + rows]
                + cw[2:3] * pltpu.roll(u, rows + 15, 0)[8:8 + rows] + cb_ref[c])
        act = (_silu(conv[:, FFN_CHUNK:]) * conv[:, :FFN_CHUNK]).astype(BF16)
        return acc + jnp.dot(act, wout_ref[c], preferred_element_type=F32)

    def chunk(c, carry):
        acc, u = carry
        u_next = up(c + 1)
        return down(c, u, acc), u_next

    acc, u_last = lax.fori_loop(0, n_chunks - 1, chunk, (jnp.zeros(h.shape, F32), up(0)), unroll=2)
    f = down(n_chunks - 1, u_last, acc)
    out_ref[0] = _layer_norm(alpha * h + gate * f, g_ref[...], b_ref[...])


def _conv_ffn(h, mods, win, cw, cb, wout, g, bb, *, alpha, ctx_len):
    b, tt, d = h.shape
    nt = tt // TILE
    halo = 8
    per = TILE // halo
    row = lambda i, t: (i, t, 0)
    return pl.pallas_call(
        functools.partial(_ffn_kernel, alpha=alpha, n_chunks=win.shape[0], first_lat=ctx_len // TILE, last=nt - 1),
        grid=(b, nt),
        in_specs=[pl.BlockSpec((1, 6, 1, d), lambda i, t: (jnp.where(t == 0, b, i), 0, 0, 0)),
                  pl.BlockSpec((1, TILE, d), row),
                  pl.BlockSpec((1, halo, d), lambda i, t: (i, jnp.maximum(t * per - 1, 0), 0)),
                  pl.BlockSpec((1, halo, d), lambda i, t: (i, jnp.minimum((t + 1) * per, tt // halo - 1), 0)),
                  _const_spec(win.shape), _const_spec(cw.shape), _const_spec(cb.shape), _const_spec(wout.shape),
                  _const_spec(g.shape), _const_spec(bb.shape)],
        out_specs=pl.BlockSpec((1, TILE, d), row),
        out_shape=jax.ShapeDtypeStruct((b, tt, d), F32),
        compiler_params=_cparams(2),
        name="conv_ffn_ln",
    )(mods, h, h, h, win, cw, cb, wout, g, bb)


def _pair_perm():
    return np.concatenate([np.arange(0, 32), np.arange(64, 96), np.arange(32, 64), np.arange(96, 128)])


def _take_cols(w, src):
    src = np.asarray(src)
    cols = jnp.take(w, jnp.asarray(np.maximum(src, 0)), axis=1)
    return jnp.where(jnp.asarray(src >= 0)[None, :], cols, 0.0)


def _axial_rope(n_tokens, rot_dim):
    t = jnp.arange(n_tokens)
    row = (t // GRID_W).astype(F32)
    col = (t % GRID_W).astype(F32)
    n_freq = rot_dim // 4
    inv_freq = ROPE_BASE ** (-jnp.arange(n_freq, dtype=F32) / n_freq)
    ang = jnp.concatenate([row[:, None] * inv_freq, col[:, None] * inv_freq], -1)
    return jnp.cos(ang), jnp.sin(ang)


def _rope_tables(seq, ctx_len, kind):
    if kind == "mla":
        cos, sin = _axial_rope(seq, MLA_ROPE)
        c = jnp.ones((seq, LANES), F32).at[:, 48:64].set(cos).at[:, 112:128].set(cos)
        s = jnp.zeros((seq, LANES), F32).at[:, 48:64].set(-sin).at[:, 112:128].set(sin)
    else:
        cos, sin = _axial_rope(seq, 64)
        c = jnp.concatenate([cos] * 4, axis=1)
        s = jnp.concatenate([-sin, -sin, sin, sin], axis=1)
    c = jnp.concatenate([jnp.ones((ctx_len, LANES), F32), c], axis=0)
    s = jnp.concatenate([jnp.zeros((ctx_len, LANES), F32), s], axis=0)
    return c, s


def _mla_head_lanes():
    nope = np.concatenate([np.arange(0, 48), np.arange(64, 80)])
    rope_a, rope_b = np.arange(48, 64), np.arange(112, 128)
    return nope, rope_a, rope_b


def _mla_weights(w_in, w_qb, w_kvb):
    nope, rope_a, rope_b = _mla_head_lanes()
    dq = MLA_NOPE + MLA_ROPE
    src_pe = np.full(LANES, -1)
    src_pe[rope_a] = MLA_QL + MLA_KVL + np.arange(16)
    src_pe[rope_b] = MLA_QL + MLA_KVL + 16 + np.arange(16)
    w1 = jnp.concatenate([w_in[:, :MLA_QL + MLA_KVL], _take_cols(w_in, src_pe)], axis=1)
    src_q = np.full(MLA_HEADS * LANES, -1)
    src_k = np.full(MLA_HEADS * LANES, -1)
    src_v = np.zeros(MLA_HEADS * MLA_V, np.int64)
    for hd in range(MLA_HEADS):
        src_q[hd * LANES + nope] = hd * dq + np.arange(MLA_NOPE)
        src_q[hd * LANES + rope_a] = hd * dq + MLA_NOPE + np.arange(16)
        src_q[hd * LANES + rope_b] = hd * dq + MLA_NOPE + 16 + np.arange(16)
        src_k[hd * LANES + nope] = hd * (MLA_NOPE + MLA_V) + np.arange(MLA_NOPE)
        src_v[hd * MLA_V:(hd + 1) * MLA_V] = hd * (MLA_NOPE + MLA_V) + MLA_NOPE + np.arange(MLA_V)
    wqb = _take_cols(w_qb, src_q) * (dq ** -0.5 * LOG2E)
    wkvb = jnp.concatenate([_take_cols(w_kvb, src_k), _take_cols(w_kvb, src_v)], axis=1)
    return w1.astype(BF16), wqb.astype(BF16), wkvb.astype(BF16)


def _swa_layout():
    rep = SWA_HEADS // SWA_KV
    head_a = [2 * (j // rep) * rep + j % rep for j in range(SWA_HEADS // 2)]
    head_b = [(2 * (j // rep) + 1) * rep + j % rep for j in range(SWA_HEADS // 2)]
    return head_a, head_b


def kernel(x, c, ctx, c_ctx, ada_w, ada_b, ln1_g, ln1_b, ln2_g, ln2_b, ffn_w_in, ffn_conv_w, ffn_conv_b, ffn_w_out, mla_w_in, mla_q_norm, mla_kv_norm, mla_w_qb, mla_w_kvb, mla_w_o, na_w_qkv, na_rpb, na_w_o, diff_w_qkv, diff_lambda, diff_subln, diff_w_o, swa_w_qkv, swa_sinks, swa_w_o):
    b, seq, d = x.shape
    ctx_len = ctx.shape[1]
    depth = ada_w.shape[0]
    assert ctx_len == TILE and seq % DENSE_TK == 0 and seq % (2 * GRID_W) == 0 and b + 1 <= MOD_ROWS
    alpha = (2.0 * depth) ** 0.25
    n_rows = seq // GRID_W
    perm = _pair_perm()

    cc = jnp.concatenate([c, c_ctx[None, :], jnp.zeros((MOD_ROWS - b - 1, d), F32)], axis=0)
    mods_all = _adaln(cc, ada_w, ada_b)
    h = jnp.concatenate([ctx, x], axis=1)

    f_hidden = ffn_w_out.shape[1]
    n_chunks = f_hidden // FFN_CHUNK

    for i in range(depth):
        mods = mods_all[i]
        kind, slot = i % 4, i // 4
        if kind == 0:
            w1, wqb, wkvb = _mla_weights(mla_w_in[slot], mla_w_qb[slot], mla_w_kvb[slot])
            cos, sin = _rope_tables(seq, ctx_len, "mla")
            qr, qu, k, v = _mla_proj(h, mods, w1, mla_q_norm[slot][None, :], mla_kv_norm[slot][None, :],
                                     wqb, wkvb, cos, sin)
            o = _dense_attn(qr, qu, k, v, kind="mla", ctx_len=ctx_len)
            w_o = mla_w_o[slot]
        elif kind == 1:
            nq = NA_HEADS * NA_DIM
            w = na_w_qkv[slot]
            w = jnp.concatenate([w[:, :nq] * (NA_DIM ** -0.5 * LOG2E), w[:, nq:]], axis=1).astype(BF16)
            q, k, v = _qkv_proj(h, mods, w, ((0, nq, False), (nq, nq, False), (2 * nq, nq, False)), name="na_proj")
            o = _na_attn(q, k, v, _na_bias_table(na_rpb[slot], n_rows), ctx_len=ctx_len)
            w_o = na_w_o[slot]
        elif kind == 2:
            nq = 2 * DIFF_HEADS * DIFF_DIM
            src = np.concatenate([g * LANES + perm for g in range(2 * nq // LANES)] + [np.arange(2 * nq, 3 * nq)])
            w = _take_cols(diff_w_qkv[slot], src)
            w = jnp.concatenate([w[:, :nq] * (DIFF_DIM ** -0.5 * LOG2E), w[:, nq:]], axis=1).astype(BF16)
            cos, sin = _rope_tables(seq, ctx_len, "pair")
            qr, qu, k, v = _qkv_proj(h, mods, w, ((0, nq, True), (0, nq, False), (nq, nq, True), (2 * nq, nq, False)),
                                     cos, sin, name="diff_proj")
            lam_init = 0.8 - 0.6 * math.exp(-0.3 * i)
            o = _dense_attn(qr, qu, k, v, kind="diff", ctx_len=ctx_len, lam=diff_lambda[slot],
                            subln=diff_subln[slot][:, None], lam_init=lam_init)
            w_o = diff_w_o[slot]
        else:
            nq, nkv = SWA_HEADS * SWA_DIM, SWA_KV * SWA_DIM
            head_a, head_b = _swa_layout()
            pair_src = lambda ha, hb, base: base + np.concatenate([ha * 64 + np.arange(64), hb * 64 + np.arange(64)])[perm]
            src = np.concatenate([pair_src(ha, hb, 0) for ha, hb in zip(head_a, head_b)]
                                 + [pair_src(2 * g, 2 * g + 1, nq) for g in range(SWA_KV // 2)]
                                 + [np.arange(nq + nkv, nq + 2 * nkv)])
            w = _take_cols(swa_w_qkv[slot], src)
            w = jnp.concatenate([w[:, :nq] * (SWA_DIM ** -0.5 * LOG2E), w[:, nq:]], axis=1).astype(BF16)
            cos, sin = _rope_tables(seq, ctx_len, "pair")
            qr, qu, k, v = _qkv_proj(h, mods, w, ((0, nq, True), (0, nq, False), (nq, nkv, True), (nq + nkv, nkv, False)),
                                     cos, sin, name="swa_proj")
            order = np.array([hd for pair in zip(head_a, head_b) for hd in pair])
            per_group = len(order) // (SWA_KV // 2)
            sinks = jnp.broadcast_to((swa_sinks[slot].astype(F32) * LOG2E)[order].reshape(-1, per_group, 1),
                                     (SWA_KV // 2, per_group, LANES))
            o = _swa_attn(qr, qu, k, v, sinks, ctx_len=ctx_len)
            rows = np.concatenate([hd * 64 + np.arange(64) for hd in order])
            w_o = jnp.take(swa_w_o[slot], jnp.asarray(rows), axis=0)
        h = _out_proj(o, w_o.astype(BF16), h, mods, ln1_g[i][None, :], ln1_b[i][None, :], alpha=alpha)

        win = ffn_w_in[i].reshape(d, 2, n_chunks, FFN_CHUNK).transpose(2, 0, 1, 3).reshape(n_chunks, d, 2 * FFN_CHUNK)
        cw = ffn_conv_w[i].reshape(3, 2, n_chunks, FFN_CHUNK).transpose(2, 0, 1, 3).reshape(n_chunks, 3, 2 * FFN_CHUNK)
        cb = ffn_conv_b[i].reshape(2, n_chunks, FFN_CHUNK).transpose(1, 0, 2).reshape(n_chunks, 1, 2 * FFN_CHUNK)
        wout = ffn_w_out[i].reshape(n_chunks, FFN_CHUNK, d)
        h = _conv_ffn(h, mods, win.astype(BF16), cw, cb, wout.astype(BF16), ln2_g[i][None, :], ln2_b[i][None, :],
                      alpha=alpha, ctx_len=ctx_len)
    return h[:, ctx_len:, :]
```

```python
import functools
import math

import numpy as np
import jax
import jax.numpy as jnp
from jax import lax
from jax.experimental import pallas as pl
from jax.experimental.pallas import tpu as pltpu

F32 = jnp.float32
BF16 = jnp.bfloat16

GRID_W = 64
ROPE_BASE = 10000.0
LN_EPS = 1e-5
RMS_EPS = 1e-6
NEG_INF = -1e30
LOG2E = 1.4426950408889634

MLA_HEADS, MLA_NOPE, MLA_ROPE, MLA_V, MLA_QL, MLA_KVL = 16, 64, 32, 64, 384, 256
NA_HEADS, NA_DIM, NA_ROWS, NA_COLS = 16, 64, 8, 16
DIFF_HEADS, DIFF_DIM = 8, 64
SWA_HEADS, SWA_KV, SWA_DIM, SWA_WINDOW = 16, 4, 64, 128
FFN_CHUNK = 256

LANES = 128
TILE = 256
NA_TQ = 128
NA_KROWS = 10
NA_GROUPS = 4
DENSE_TK = 2048
MOD_ROWS = 24
VMEM_LIMIT = 56 * 1024 * 1024


def _cparams(n_axes):
    return pltpu.CompilerParams(dimension_semantics=("arbitrary",) * n_axes, vmem_limit_bytes=VMEM_LIMIT)


def _const_spec(shape):
    n = len(shape)
    return pl.BlockSpec(shape, lambda *_: (0,) * n, pipeline_mode=pl.Buffered(1))


def _silu(x):
    return x * (1.0 / (1.0 + jnp.exp(-x)))


def _layer_norm(x, g, b):
    mu = jnp.mean(x, axis=-1, keepdims=True)
    xc = x - mu
    var = jnp.mean(xc * xc, axis=-1, keepdims=True)
    return xc * lax.rsqrt(var + LN_EPS) * g + b


def _rms_norm(x, g):
    return x * lax.rsqrt(jnp.mean(x * x, axis=-1, keepdims=True) + RMS_EPS) * g


def _qk(q, k):
    return lax.dot_general(q, k, (((1,), (1,)), ((), ())), preferred_element_type=F32)


def _rope(slab, cos, sin):
    return slab * cos + pltpu.roll(slab, 64, 1) * sin


def _adaln_kernel(c_ref, w_ref, b_ref, o_ref):
    sc = _silu(c_ref[...])
    o_ref[0] = jnp.dot(sc, w_ref[0], preferred_element_type=F32, precision=lax.Precision.HIGHEST) + b_ref[0]


def _adaln(cc, ada_w, ada_b):
    depth, d, n = ada_w.shape
    tn = 1024
    out = pl.pallas_call(
        _adaln_kernel,
        grid=(depth, n // tn),
        in_specs=[pl.BlockSpec((MOD_ROWS, d), lambda l, j: (0, 0)),
                  pl.BlockSpec((1, d, tn), lambda l, j: (l, 0, j)),
                  pl.BlockSpec((1, 1, tn), lambda l, j: (l, 0, j))],
        out_specs=pl.BlockSpec((1, MOD_ROWS, tn), lambda l, j: (l, 0, j)),
        out_shape=jax.ShapeDtypeStruct((depth, MOD_ROWS, n), F32),
        compiler_params=_cparams(2),
        name="adaln",
    )(cc, ada_w, ada_b.reshape(depth, 1, n))
    return out.reshape(depth, MOD_ROWS, 6, 1, d)


def _qkv_kernel(*refs, plan, use_rope):
    if use_rope:
        mod_ref, x_ref, w_ref, cos_ref, sin_ref = refs[:5]
        out_refs = refs[5:]
        cos, sin = cos_ref[...], sin_ref[...]
    else:
        mod_ref, x_ref, w_ref = refs[:3]
        out_refs = refs[3:]
    sh, sc = mod_ref[0, 0], mod_ref[0, 1]
    a = (x_ref[0] * (1.0 + sc) + sh).astype(BF16)
    y = jnp.dot(a, w_ref[...], preferred_element_type=F32)
    for o_ref, (c0, width, rope) in zip(out_refs, plan):
        if rope:
            for j in range(width // LANES):
                slab = y[:, c0 + LANES * j:c0 + LANES * (j + 1)]
                o_ref[0, :, LANES * j:LANES * (j + 1)] = _rope(slab, cos, sin).astype(BF16)
        else:
            o_ref[0] = y[:, c0:c0 + width].astype(BF16)


def _qkv_proj(h, mods, w, plan, cos=None, sin=None, *, name):
    b, tt, d = h.shape
    n = w.shape[1]
    use_rope = cos is not None
    nt = tt // TILE
    in_specs = [pl.BlockSpec((1, 6, 1, d), lambda i, t: (jnp.where(t == 0, b, i), 0, 0, 0)),
                pl.BlockSpec((1, TILE, d), lambda i, t: (i, t, 0)),
                _const_spec((d, n))]
    args = [mods, h, w]
    if use_rope:
        in_specs += [pl.BlockSpec((TILE, LANES), lambda i, t: (t, 0))] * 2
        args += [cos, sin]
    return pl.pallas_call(
        functools.partial(_qkv_kernel, plan=plan, use_rope=use_rope),
        grid=(b, nt),
        in_specs=in_specs,
        out_specs=[pl.BlockSpec((1, TILE, wd), lambda i, t: (i, t, 0)) for _, wd, _ in plan],
        out_shape=[jax.ShapeDtypeStruct((b, tt, wd), BF16) for _, wd, _ in plan],
        compiler_params=_cparams(2),
        name=name,
    )(*args)


def _mla_proj_kernel(mod_ref, x_ref, w1_ref, qn_ref, kvn_ref, wqb_ref, wkvb_ref, cos_ref, sin_ref,
                     qr_ref, qu_ref, k_ref, v_ref):
    sh, sc = mod_ref[0, 0], mod_ref[0, 1]
    cos, sin = cos_ref[...], sin_ref[...]
    a = (x_ref[0] * (1.0 + sc) + sh).astype(BF16)
    y1 = jnp.dot(a, w1_ref[...], preferred_element_type=F32)
    cq = _rms_norm(y1[:, :MLA_QL], qn_ref[...]).astype(BF16)
    ckv = _rms_norm(y1[:, MLA_QL:MLA_QL + MLA_KVL], kvn_ref[...]).astype(BF16)
    pe = _rope(y1[:, MLA_QL + MLA_KVL:], cos, sin)
    q = jnp.dot(cq, wqb_ref[...], preferred_element_type=F32)
    kv = jnp.dot(ckv, wkvb_ref[...], preferred_element_type=F32)
    for hd in range(MLA_HEADS):
        sl = slice(LANES * hd, LANES * (hd + 1))
        qs = q[:, sl]
        qu_ref[0, :, sl] = qs.astype(BF16)
        qr_ref[0, :, sl] = _rope(qs, cos, sin).astype(BF16)
        k_ref[0, :, sl] = (kv[:, sl] + pe).astype(BF16)
    v_ref[0] = kv[:, MLA_HEADS * LANES:].astype(BF16)


def _mla_proj(h, mods, w1, qn, kvn, wqb, wkvb, cos, sin):
    b, tt, d = h.shape
    nt = tt // TILE
    hw = MLA_HEADS * LANES
    vw = MLA_HEADS * MLA_V
    row = lambda i, t: (i, t, 0)
    return pl.pallas_call(
        _mla_proj_kernel,
        grid=(b, nt),
        in_specs=[pl.BlockSpec((1, 6, 1, d), lambda i, t: (jnp.where(t == 0, b, i), 0, 0, 0)),
                  pl.BlockSpec((1, TILE, d), row),
                  _const_spec(w1.shape), _const_spec(qn.shape), _const_spec(kvn.shape),
                  _const_spec(wqb.shape), _const_spec(wkvb.shape),
                  pl.BlockSpec((TILE, LANES), lambda i, t: (t, 0)),
                  pl.BlockSpec((TILE, LANES), lambda i, t: (t, 0))],
        out_specs=[pl.BlockSpec((1, TILE, hw), row), pl.BlockSpec((1, TILE, hw), row),
                   pl.BlockSpec((1, TILE, hw), row), pl.BlockSpec((1, TILE, vw), row)],
        out_shape=[jax.ShapeDtypeStruct((b, tt, hw), BF16)] * 3 + [jax.ShapeDtypeStruct((b, tt, vw), BF16)],
        compiler_params=_cparams(2),
        name="mla_proj",
    )(mods, h, w1, qn, kvn, wqb, wkvb, cos, sin)


def _dense_attn_kernel(*refs, kind, ctx_len, tk, n_lat, lam_init):
    if kind == "mla":
        qr_ref, qu_ref, k_ref, vt_ref, o_ref, m_sc, l_sc, acc_sc = refs
    else:
        qr_ref, qu_ref, k_ref, vt_ref, lam_ref, sub_ref, o_ref, m_sc, l_sc, acc_sc = refs
    t = pl.program_id(2)
    qw = k_ref.shape[2]
    lane = lax.broadcasted_iota(jnp.int32, (1, qw), 1)
    sub_tiles = tk // TILE
    tq = qr_ref.shape[1]

    def queries(q_ref):
        q = q_ref[0]
        own = (lane < LANES) if kind == "mla" else ((lane // 32) % 2 == 0)
        zero = jnp.zeros_like(q)
        return jnp.concatenate([jnp.where(own, q, zero), jnp.where(own, zero, q)], axis=0)

    st = _qk(k_ref[0, 0:ctx_len, :], queries(qu_ref))
    m = jnp.max(st, axis=0, keepdims=True)
    p = jnp.exp2(st - m)
    m_sc[...] = m
    l_sc[...] = jnp.sum(p, axis=0, keepdims=True)
    acc_sc[...] = jnp.dot(vt_ref[0, 0, 0], p.astype(BF16), preferred_element_type=F32)

    @pl.when(t > 0)
    def _latent():
        q_rot = queries(qr_ref)

        def body(i, carry):
            m0, l0, acc0 = carry
            start = pl.multiple_of(ctx_len + i * tk, TILE)
            st = _qk(k_ref[0, pl.ds(start, tk), :], q_rot)
            m1 = jnp.maximum(m0, jnp.max(st, axis=0, keepdims=True))
            alpha = jnp.exp2(m0 - m1)
            p = jnp.exp2(st - m1)
            l1 = alpha * l0 + jnp.sum(p, axis=0, keepdims=True)
            pb = p.astype(BF16)
            pv = None
            for j in range(sub_tiles):
                d = jnp.dot(vt_ref[0, 0, 1 + i * sub_tiles + j], pb[TILE * j:TILE * (j + 1)],
                            preferred_element_type=F32)
                pv = d if pv is None else pv + d
            return m1, l1, alpha * acc0 + pv

        m2, l2, acc2 = lax.fori_loop(0, n_lat, body, (m_sc[...], l_sc[...], acc_sc[...]), unroll=True)
        m_sc[...] = m2
        l_sc[...] = l2
        acc_sc[...] = acc2

    o = acc_sc[...] / l_sc[...]
    o0, o1 = o[:, :tq], o[:, tq:]
    if kind == "mla":
        o = jnp.concatenate([o0[:64], o1[64:]], axis=0)
    else:
        lf = lam_ref[...]
        lam = (jnp.exp(jnp.sum(lf[0:1] * lf[1:2], axis=1, keepdims=True))
               - jnp.exp(jnp.sum(lf[2:3] * lf[3:4], axis=1, keepdims=True)) + lam_init)
        o = o0 - lam * o1
        o = o * lax.rsqrt(jnp.mean(o * o, axis=0, keepdims=True) + RMS_EPS) * sub_ref[...] * (1.0 - lam_init)
    o_ref[0] = o.T.astype(BF16)


def _dense_attn(qr, qu, k, v, *, kind, ctx_len, lam=None, subln=None, lam_init=0.0):
    b, tt, _ = v.shape
    groups = v.shape[2] // LANES
    qw = qr.shape[2] // groups
    nt = tt // TILE
    tk = min(DENSE_TK, tt - ctx_len)
    n_lat = (tt - ctx_len) // tk
    vt = v.reshape(b, nt, TILE, groups, LANES).transpose(0, 3, 1, 4, 2)
    in_specs = [pl.BlockSpec((1, TILE, qw), lambda i, g, t: (i, t, g)),
                pl.BlockSpec((1, TILE, qw), lambda i, g, t: (i, t, g)),
                pl.BlockSpec((1, tt, qw), lambda i, g, t: (i, 0, g)),
                pl.BlockSpec((1, 1, nt, LANES, TILE), lambda i, g, t: (i, g, 0, 0, 0))]
    args = [qr, qu, k, vt]
    if kind == "diff":
        in_specs += [_const_spec(lam.shape), _const_spec(subln.shape)]
        args += [lam, subln]
    return pl.pallas_call(
        functools.partial(_dense_attn_kernel, kind=kind, ctx_len=ctx_len, tk=tk, n_lat=n_lat, lam_init=lam_init),
        grid=(b, groups, nt),
        in_specs=in_specs,
        out_specs=pl.BlockSpec((1, TILE, LANES), lambda i, g, t: (i, t, g)),
        out_shape=jax.ShapeDtypeStruct((b, tt, groups * LANES), BF16),
        scratch_shapes=[pltpu.VMEM((1, 2 * TILE), F32), pltpu.VMEM((1, 2 * TILE), F32),
                        pltpu.VMEM((LANES, 2 * TILE), F32)],
        compiler_params=_cparams(3),
        name=kind + "_attn",
    )(*args)


def _softmax_pv(segments, sink=None):
    m = None
    for s, _ in segments:
        ms = jnp.max(s, axis=1, keepdims=True)
        m = ms if m is None else jnp.maximum(m, ms)
    if sink is not None:
        m = jnp.maximum(m, sink)
    l = jnp.exp2(sink - m) if sink is not None else 0.0
    acc = None
    for s, v in segments:
        p = jnp.exp2(s - m)
        l = l + jnp.sum(p, axis=1, keepdims=True)
        pv = jnp.dot(p.astype(BF16), v, preferred_element_type=F32)
        acc = pv if acc is None else acc + pv
    return acc / l


def _na_attn_kernel(q_ref, k_ref, v_ref, bias_ref, o_ref, *, ctx_len, n_rows):
    t = pl.program_id(2)
    n_ctx_tiles = ctx_len // NA_TQ
    lane = lax.broadcasted_iota(jnp.int32, (1, LANES), 1)
    nk = NA_KROWS * GRID_W

    def head_pair(i):
        sl = slice(LANES * i, LANES * (i + 1))
        q = q_ref[0, :, sl]
        return sl, [jnp.where(lane < 64, q, jnp.zeros_like(q)), jnp.where(lane >= 64, q, jnp.zeros_like(q))]

    @pl.when(t < n_ctx_tiles)
    def _ctx():
        for i in range(q_ref.shape[2] // LANES):
            sl, qs = head_pair(i)
            kc, vc = k_ref[0, 0:ctx_len, sl], v_ref[0, 0:ctx_len, sl]
            o = [_softmax_pv([(_qk(qs[s], kc), vc)]) for s in range(2)]
            o_ref[0, :, sl] = jnp.where(lane < 64, o[0], o[1]).astype(BF16)

    @pl.when(t >= n_ctx_tiles)
    def _lat():
        j = t - n_ctx_tiles
        row0 = jnp.clip(2 * j - NA_ROWS // 2, 0, n_rows - NA_KROWS)
        start = pl.multiple_of(ctx_len + row0 * GRID_W, GRID_W)
        for i in range(q_ref.shape[2] // LANES):
            sl, qs = head_pair(i)
            kc, vc = k_ref[0, 0:ctx_len, sl], v_ref[0, 0:ctx_len, sl]
            kb, vb = k_ref[0, pl.ds(start, nk), sl], v_ref[0, pl.ds(start, nk), sl]
            o = [_softmax_pv([(_qk(qs[s], kc), vc), (_qk(qs[s], kb) + bias_ref[0, 2 * i + s], vb)])
                 for s in range(2)]
            o_ref[0, :, sl] = jnp.where(lane < 64, o[0], o[1]).astype(BF16)


def _na_tile_class(j, n_tiles):
    return jnp.where(j < 2, j, jnp.where(j >= n_tiles - 2, j - (n_tiles - 2) + 3, 2))


def _na_attn(q, k, v, bias, *, ctx_len):
    b, tt, w = q.shape
    groups = w // LANES
    n_rows = (tt - ctx_len) // GRID_W
    n_ctx_tiles = ctx_len // NA_TQ
    n_lat_tiles = n_rows // 2
    nk = NA_KROWS * GRID_W
    gw = NA_GROUPS * LANES

    def bias_map(i, g, t):
        return (_na_tile_class(jnp.maximum(t - n_ctx_tiles, 0), n_lat_tiles), g, 0, 0)

    return pl.pallas_call(
        functools.partial(_na_attn_kernel, ctx_len=ctx_len, n_rows=n_rows),
        grid=(b, groups // NA_GROUPS, tt // NA_TQ),
        in_specs=[pl.BlockSpec((1, NA_TQ, gw), lambda i, g, t: (i, t, g)),
                  pl.BlockSpec((1, tt, gw), lambda i, g, t: (i, 0, g)),
                  pl.BlockSpec((1, tt, gw), lambda i, g, t: (i, 0, g)),
                  pl.BlockSpec((1, 2 * NA_GROUPS, NA_TQ, nk), bias_map)],
        out_specs=pl.BlockSpec((1, NA_TQ, gw), lambda i, g, t: (i, t, g)),
        out_shape=jax.ShapeDtypeStruct((b, tt, w), BF16),
        compiler_params=_cparams(3),
        name="na_attn",
    )(q, k, v, bias)


def _na_bias_table(rpb, n_rows):
    heads, n_ri, n_ci = rpb.shape
    n_tiles = n_rows // 2
    w = jnp.pad(rpb.astype(F32) * LOG2E, ((0, 0), (0, 0), (0, LANES - n_ci)))
    r = jnp.broadcast_to(w[:, :, None, :], (heads, n_ri, GRID_W, LANES)).reshape(heads, n_ri, GRID_W * LANES)
    r = r[:, :, :GRID_W * (LANES - 1)].reshape(heads, n_ri, GRID_W, LANES - 1)
    toep = r[..., NA_COLS - 1:NA_COLS - 1 + GRID_W]
    c = np.arange(GRID_W)
    cs = np.clip(c - NA_COLS // 2, 0, GRID_W - NA_COLS)
    col_ok = (c[None, :] >= cs[:, None]) & (c[None, :] < cs[:, None] + NA_COLS)
    toep = jnp.where(jnp.asarray(col_ok), toep, NEG_INF)
    masked = jnp.full((heads, GRID_W, GRID_W), NEG_INF, F32)
    classes = []
    for rep in (0, 1, 2, n_tiles - 2, n_tiles - 1):
        row0 = int(np.clip(2 * rep - NA_ROWS // 2, 0, n_rows - NA_KROWS))
        q_rows = []
        for a in range(2):
            qr = 2 * rep + a
            r0 = int(np.clip(qr - NA_ROWS // 2, 0, n_rows - NA_ROWS))
            blocks = [toep[:, kr - qr + NA_ROWS - 1] if r0 <= kr < r0 + NA_ROWS else masked
                      for kr in range(row0, row0 + NA_KROWS)]
            q_rows.append(jnp.concatenate(blocks, axis=-1))
        classes.append(jnp.concatenate(q_rows, axis=1))
    return jnp.stack(classes)


def _swa_attn_kernel(qr_ref, qu_ref, k_ref, v_ref, sink_ref, o_ref, *, ctx_len, seq):
    t = pl.program_id(2)
    band = TILE + 2 * SWA_WINDOW
    lane = lax.broadcasted_iota(jnp.int32, (1, LANES), 1)
    kc, vc = k_ref[0, 0:ctx_len, :], v_ref[0, 0:ctx_len, :]
    n_slabs = qr_ref.shape[2] // LANES

    def heads(q_ref):
        out = []
        for i in range(n_slabs):
            slab = q_ref[0, :, LANES * i:LANES * (i + 1)]
            for s in range(2):
                out.append(jnp.where(((lane // 32) % 2) == s, slab, jnp.zeros_like(slab)))
        return out

    def sink(idx):
        return sink_ref[0, idx:idx + 1, 0:1]

    def store(outs):
        for i in range(n_slabs):
            o_ref[0, :, LANES * i:LANES * (i + 1)] = jnp.where(lane < 64, outs[2 * i], outs[2 * i + 1]).astype(BF16)

    @pl.when(t == 0)
    def _ctx():
        store([_softmax_pv([(_qk(q, kc), vc)], sink(n)) for n, q in enumerate(heads(qu_ref))])

    @pl.when(t > 0)
    def _lat():
        p0 = (t - 1) * TILE
        rel = jnp.clip(p0 - SWA_WINDOW, 0, seq - band)
        start = pl.multiple_of(ctx_len + rel, SWA_WINDOW)
        kb = k_ref[0, pl.ds(start, band), :]
        vb = v_ref[0, pl.ds(start, band), :]
        qpos = p0 + lax.broadcasted_iota(jnp.int32, (TILE, band), 0)
        kpos = rel + lax.broadcasted_iota(jnp.int32, (TILE, band), 1)
        valid = jnp.abs(qpos - kpos) <= SWA_WINDOW
        outs = []
        for n, (qu, qr) in enumerate(zip(heads(qu_ref), heads(qr_ref))):
            s_l = jnp.where(valid, _qk(qr, kb), NEG_INF)
            outs.append(_softmax_pv([(_qk(qu, kc), vc), (s_l, vb)], sink(n)))
        store(outs)


def _swa_attn(qr, qu, k, v, sinks, *, ctx_len):
    b, tt, w = qr.shape
    groups = k.shape[2] // LANES
    qw = w // groups
    return pl.pallas_call(
        functools.partial(_swa_attn_kernel, ctx_len=ctx_len, seq=tt - ctx_len),
        grid=(b, groups, tt // TILE),
        in_specs=[pl.BlockSpec((1, TILE, qw), lambda i, g, t: (i, t, g)),
                  pl.BlockSpec((1, TILE, qw), lambda i, g, t: (i, t, g)),
                  pl.BlockSpec((1, tt, LANES), lambda i, g, t: (i, 0, g)),
                  pl.BlockSpec((1, tt, LANES), lambda i, g, t: (i, 0, g)),
                  pl.BlockSpec((1, sinks.shape[1], LANES), lambda i, g, t: (g, 0, 0))],
        out_specs=pl.BlockSpec((1, TILE, qw), lambda i, g, t: (i, t, g)),
        out_shape=jax.ShapeDtypeStruct((b, tt, w), BF16),
        compiler_params=_cparams(3),
        name="swa_attn",
    )(qr, qu, k, v, sinks)


def _out_kernel(mod_ref, o_ref, w_ref, h_ref, g_ref, b_ref, out_ref, *, alpha):
    y = jnp.dot(o_ref[0], w_ref[...], preferred_element_type=F32)
    out_ref[0] = _layer_norm(alpha * h_ref[0] + mod_ref[0, 2] * y, g_ref[...], b_ref[...])


def _out_proj(o, w_o, h, mods, g, bb, *, alpha):
    b, tt, d = h.shape
    row = lambda i, t: (i, t, 0)
    return pl.pallas_call(
        functools.partial(_out_kernel, alpha=alpha),
        grid=(b, tt // TILE),
        in_specs=[pl.BlockSpec((1, 6, 1, d), lambda i, t: (jnp.where(t == 0, b, i), 0, 0, 0)),
                  pl.BlockSpec((1, TILE, o.shape[2]), row),
                  _const_spec(w_o.shape),
                  pl.BlockSpec((1, TILE, d), row),
                  _const_spec(g.shape), _const_spec(bb.shape)],
        out_specs=pl.BlockSpec((1, TILE, d), row),
        out_shape=jax.ShapeDtypeStruct((b, tt, d), F32),
        compiler_params=_cparams(2),
        name="out_proj_ln",
    )(mods, o, w_o, h, g, bb)


def _ffn_kernel(mod_ref, h_ref, hp_ref, hn_ref, win_ref, cw_ref, cb_ref, wout_ref, g_ref, b_ref, out_ref,
                *, alpha, n_chunks, first_lat, last):
    t = pl.program_id(1)
    sh, sc, gate = mod_ref[0, 3], mod_ref[0, 4], mod_ref[0, 5]
    h = h_ref[0]
    has_prev = jnp.where((t == 0) | (t == first_lat), 0.0, 1.0)
    has_next = jnp.where((t == first_lat - 1) | (t == last), 0.0, 1.0)
    mod = lambda x: x * (1.0 + sc) + sh
    xe = jnp.concatenate([mod(hp_ref[0]) * has_prev, mod(h), mod(hn_ref[0]) * has_next], axis=0).astype(BF16)
    rows = h.shape[0]

    def up(c):
        return jnp.dot(xe, win_ref[c], preferred_element_type=F32)

    def down(c, u, acc):
        cw = cw_ref[c]
        conv = (cw[0:1] * pltpu.roll(u, 1, 0)[8:8 + rows] + cw[1:2] * u[8:8 + rows]
                + cw[2:3] * pltpu.roll(u, rows + 15, 0)[8:8 + rows] + cb_ref[c])
        act = (_silu(conv[:, FFN_CHUNK:]) * conv[:, :FFN_CHUNK]).astype(BF16)
        return acc + jnp.dot(act, wout_ref[c], preferred_element_type=F32)

    def chunk(c, carry):
        acc, u = carry
        u_next = up(c + 1)
        return down(c, u, acc), u_next

    acc, u_last = lax.fori_loop(0, n_chunks - 1, chunk, (jnp.zeros(h.shape, F32), up(0)), unroll=True)
    f = down(n_chunks - 1, u_last, acc)
    out_ref[0] = _layer_norm(alpha * h + gate * f, g_ref[...], b_ref[...])


def _conv_ffn(h, mods, win, cw, cb, wout, g, bb, *, alpha, ctx_len):
    b, tt, d = h.shape
    nt = tt // TILE
    halo = 8
    per = TILE // halo
    row = lambda i, t: (i, t, 0)
    return pl.pallas_call(
        functools.partial(_ffn_kernel, alpha=alpha, n_chunks=win.shape[0], first_lat=ctx_len // TILE, last=nt - 1),
        grid=(b, nt),
        in_specs=[pl.BlockSpec((1, 6, 1, d), lambda i, t: (jnp.where(t == 0, b, i), 0, 0, 0)),
                  pl.BlockSpec((1, TILE, d), row),
                  pl.BlockSpec((1, halo, d), lambda i, t: (i, jnp.maximum(t * per - 1, 0), 0)),
                  pl.BlockSpec((1, halo, d), lambda i, t: (i, jnp.minimum((t + 1) * per, tt // halo - 1), 0)),
                  _const_spec(win.shape), _const_spec(cw.shape), _const_spec(cb.shape), _const_spec(wout.shape),
                  _const_spec(g.shape), _const_spec(bb.shape)],
        out_specs=pl.BlockSpec((1, TILE, d), row),
        out_shape=jax.ShapeDtypeStruct((b, tt, d), F32),
        compiler_params=_cparams(2),
        name="conv_ffn_ln",
    )(mods, h, h, h, win, cw, cb, wout, g, bb)


def _pair_perm():
    return np.concatenate([np.arange(0, 32), np.arange(64, 96), np.arange(32, 64), np.arange(96, 128)])


def _take_cols(w, src):
    src = np.asarray(src)
    cols = jnp.take(w, jnp.asarray(np.maximum(src, 0)), axis=1)
    return jnp.where(jnp.asarray(src >= 0)[None, :], cols, 0.0)


def _axial_rope(n_tokens, rot_dim):
    t = jnp.arange(n_tokens)
    row = (t // GRID_W).astype(F32)
    col = (t % GRID_W).astype(F32)
    n_freq = rot_dim // 4
    inv_freq = ROPE_BASE ** (-jnp.arange(n_freq, dtype=F32) / n_freq)
    ang = jnp.concatenate([row[:, None] * inv_freq, col[:, None] * inv_freq], -1)
    return jnp.cos(ang), jnp.sin(ang)


def _rope_tables(seq, ctx_len, kind):
    if kind == "mla":
        cos, sin = _axial_rope(seq, MLA_ROPE)
        c = jnp.ones((seq, LANES), F32).at[:, 48:64].set(cos).at[:, 112:128].set(cos)
        s = jnp.zeros((seq, LANES), F32).at[:, 48:64].set(-sin).at[:, 112:128].set(sin)
    else:
        cos, sin = _axial_rope(seq, 64)
        c = jnp.concatenate([cos] * 4, axis=1)
        s = jnp.concatenate([-sin, -sin, sin, sin], axis=1)
    c = jnp.concatenate([jnp.ones((ctx_len, LANES), F32), c], axis=0)
    s = jnp.concatenate([jnp.zeros((ctx_len, LANES), F32), s], axis=0)
    return c, s


def _mla_head_lanes():
    nope = np.concatenate([np.arange(0, 48), np.arange(64, 80)])
    rope_a, rope_b = np.arange(48, 64), np.arange(112, 128)
    return nope, rope_a, rope_b


def _mla_weights(w_in, w_qb, w_kvb):
    nope, rope_a, rope_b = _mla_head_lanes()
    dq = MLA_NOPE + MLA_ROPE
    src_pe = np.full(LANES, -1)
    src_pe[rope_a] = MLA_QL + MLA_KVL + np.arange(16)
    src_pe[rope_b] = MLA_QL + MLA_KVL + 16 + np.arange(16)
    w1 = jnp.concatenate([w_in[:, :MLA_QL + MLA_KVL], _take_cols(w_in, src_pe)], axis=1)
    src_q = np.full(MLA_HEADS * LANES, -1)
    src_k = np.full(MLA_HEADS * LANES, -1)
    src_v = np.zeros(MLA_HEADS * MLA_V, np.int64)
    for hd in range(MLA_HEADS):
        src_q[hd * LANES + nope] = hd * dq + np.arange(MLA_NOPE)
        src_q[hd * LANES + rope_a] = hd * dq + MLA_NOPE + np.arange(16)
        src_q[hd * LANES + rope_b] = hd * dq + MLA_NOPE + 16 + np.arange(16)
        src_k[hd * LANES + nope] = hd * (MLA_NOPE + MLA_V) + np.arange(MLA_NOPE)
        src_v[hd * MLA_V:(hd + 1) * MLA_V] = hd * (MLA_NOPE + MLA_V) + MLA_NOPE + np.arange(MLA_V)
    wqb = _take_cols(w_qb, src_q) * (dq ** -0.5 * LOG2E)
    wkvb = jnp.concatenate([_take_cols(w_kvb, src_k), _take_cols(w_kvb, src_v)], axis=1)
    return w1.astype(BF16), wqb.astype(BF16), wkvb.astype(BF16)


def _swa_layout():
    rep = SWA_HEADS // SWA_KV
    head_a = [2 * (j // rep) * rep + j % rep for j in range(SWA_HEADS // 2)]
    head_b = [(2 * (j // rep) + 1) * rep + j % rep for j in range(SWA_HEADS // 2)]
    return head_a, head_b


def kernel(x, c, ctx, c_ctx, ada_w, ada_b, ln1_g, ln1_b, ln2_g, ln2_b, ffn_w_in, ffn_conv_w, ffn_conv_b, ffn_w_out, mla_w_in, mla_q_norm, mla_kv_norm, mla_w_qb, mla_w_kvb, mla_w_o, na_w_qkv, na_rpb, na_w_o, diff_w_qkv, diff_lambda, diff_subln, diff_w_o, swa_w_qkv, swa_sinks, swa_w_o):
    b, seq, d = x.shape
    ctx_len = ctx.shape[1]
    depth = ada_w.shape[0]
    assert ctx_len == TILE and seq % min(DENSE_TK, seq) == 0 and seq % (2 * GRID_W) == 0 and b + 1 <= MOD_ROWS
    alpha = (2.0 * depth) ** 0.25
    n_rows = seq // GRID_W
    perm = _pair_perm()

    cc = jnp.concatenate([c, c_ctx[None, :], jnp.zeros((MOD_ROWS - b - 1, d), F32)], axis=0)
    mods_all = _adaln(cc, ada_w, ada_b)
    h = jnp.concatenate([ctx, x], axis=1)

    f_hidden = ffn_w_out.shape[1]
    n_chunks = f_hidden // FFN_CHUNK

    for i in range(depth):
        mods = mods_all[i]
        kind, slot = i % 4, i // 4
        if kind == 0:
            w1, wqb, wkvb = _mla_weights(mla_w_in[slot], mla_w_qb[slot], mla_w_kvb[slot])
            cos, sin = _rope_tables(seq, ctx_len, "mla")
            qr, qu, k, v = _mla_proj(h, mods, w1, mla_q_norm[slot][None, :], mla_kv_norm[slot][None, :],
                                     wqb, wkvb, cos, sin)
            o = _dense_attn(qr, qu, k, v, kind="mla", ctx_len=ctx_len)
            w_o = mla_w_o[slot]
        elif kind == 1:
            nq = NA_HEADS * NA_DIM
            w = na_w_qkv[slot]
            w = jnp.concatenate([w[:, :nq] * (NA_DIM ** -0.5 * LOG2E), w[:, nq:]], axis=1).astype(BF16)
            q, k, v = _qkv_proj(h, mods, w, ((0, nq, False), (nq, nq, False), (2 * nq, nq, False)), name="na_proj")
            o = _na_attn(q, k, v, _na_bias_table(na_rpb[slot], n_rows), ctx_len=ctx_len)
            w_o = na_w_o[slot]
        elif kind == 2:
            nq = 2 * DIFF_HEADS * DIFF_DIM
            src = np.concatenate([g * LANES + perm for g in range(2 * nq // LANES)] + [np.arange(2 * nq, 3 * nq)])
            w = _take_cols(diff_w_qkv[slot], src)
            w = jnp.concatenate([w[:, :nq] * (DIFF_DIM ** -0.5 * LOG2E), w[:, nq:]], axis=1).astype(BF16)
            cos, sin = _rope_tables(seq, ctx_len, "pair")
            qr, qu, k, v = _qkv_proj(h, mods, w, ((0, nq, True), (0, nq, False), (nq, nq, True), (2 * nq, nq, False)),
                                     cos, sin, name="diff_proj")
            lam_init = 0.8 - 0.6 * math.exp(-0.3 * i)
            o = _dense_attn(qr, qu, k, v, kind="diff", ctx_len=ctx_len, lam=diff_lambda[slot],
                            subln=diff_subln[slot][:, None], lam_init=lam_init)
            w_o = diff_w_o[slot]
        else:
            nq, nkv = SWA_HEADS * SWA_DIM, SWA_KV * SWA_DIM
            head_a, head_b = _swa_layout()
            pair_src = lambda ha, hb, base: base + np.concatenate([ha * 64 + np.arange(64), hb * 64 + np.arange(64)])[perm]
            src = np.concatenate([pair_src(ha, hb, 0) for ha, hb in zip(head_a, head_b)]
                                 + [pair_src(2 * g, 2 * g + 1, nq) for g in range(SWA_KV // 2)]
                                 + [np.arange(nq + nkv, nq + 2 * nkv)])
            w = _take_cols(swa_w_qkv[slot], src)
            w = jnp.concatenate([w[:, :nq] * (SWA_DIM ** -0.5 * LOG2E), w[:, nq:]], axis=1).astype(BF16)
            cos, sin = _rope_tables(seq, ctx_len, "pair")
            qr, qu, k, v = _qkv_proj(h, mods, w, ((0, nq, True), (0, nq, False), (nq, nkv, True), (nq + nkv, nkv, False)),
                                     cos, sin, name="swa_proj")
            order = np.array([hd for pair in zip(head_a, head_b) for hd in pair])
            per_group = len(order) // (SWA_KV // 2)
            sinks = jnp.broadcast_to((swa_sinks[slot].astype(F32) * LOG2E)[order].reshape(-1, per_group, 1),
                                     (SWA_KV // 2, per_group, LANES))
            o = _swa_attn(qr, qu, k, v, sinks, ctx_len=ctx_len)
            rows = np.concatenate([hd * 64 + np.arange(64) for hd in order])
            w_o = jnp.take(swa_w_o[slot], jnp.asarray(rows), axis=0)
        h = _out_proj(o, w_o.astype(BF16), h, mods, ln1_g[i][None, :], ln1_b[i][None, :], alpha=alpha)

        win = ffn_w_in[i].reshape(d, 2, n_chunks, FFN_CHUNK).transpose(2, 0, 1, 3).reshape(n_chunks, d, 2 * FFN_CHUNK)
        cw = ffn_conv_w[i].reshape(3, 2, n_chunks, FFN_CHUNK).transpose(2, 0, 1, 3).reshape(n_chunks, 3, 2 * FFN_CHUNK)
        cb = ffn_conv_b[i].reshape(2, n_chunks, FFN_CHUNK).transpose(1, 0, 2).reshape(n_chunks, 1, 2 * FFN_CHUNK)
        wout = ffn_w_out[i].reshape(n_chunks, FFN_CHUNK, d)
        h = _conv_ffn(h, mods, win.astype(BF16), cw, cb, wout.astype(BF16), ln2_g[i][None, :], ln2_b[i][None, :],
                      alpha=alpha, ctx_len=ctx_len)
    return h[:, ctx_len:, :]
```

```python
import functools
import math

import numpy as np
import jax
import jax.numpy as jnp
from jax import lax
from jax.experimental import pallas as pl
from jax.experimental.pallas import tpu as pltpu

F32 = jnp.float32
BF16 = jnp.bfloat16

GRID_W = 64
ROPE_BASE = 10000.0
LN_EPS = 1e-5
RMS_EPS = 1e-6
NEG_INF = -1e30
LOG2E = 1.4426950408889634

MLA_HEADS, MLA_NOPE, MLA_ROPE, MLA_V, MLA_QL, MLA_KVL = 16, 64, 32, 64, 384, 256
NA_HEADS, NA_DIM, NA_ROWS, NA_COLS = 16, 64, 8, 16
DIFF_HEADS, DIFF_DIM = 8, 64
SWA_HEADS, SWA_KV, SWA_DIM, SWA_WINDOW = 16, 4, 64, 128
FFN_CHUNK = 256

LANES = 128
TILE = 256
NA_TQ = 128
NA_KROWS = 10
NA_GROUPS = 4
DENSE_TK = 4096
MOD_ROWS = 24
VMEM_LIMIT = 56 * 1024 * 1024


def _cparams(n_axes):
    return pltpu.CompilerParams(dimension_semantics=("arbitrary",) * n_axes, vmem_limit_bytes=VMEM_LIMIT)


def _const_spec(shape):
    n = len(shape)
    return pl.BlockSpec(shape, lambda *_: (0,) * n, pipeline_mode=pl.Buffered(1))


def _silu(x):
    return x * (1.0 / (1.0 + jnp.exp(-x)))


def _layer_norm(x, g, b):
    mu = jnp.mean(x, axis=-1, keepdims=True)
    xc = x - mu
    var = jnp.mean(xc * xc, axis=-1, keepdims=True)
    return xc * lax.rsqrt(var + LN_EPS) * g + b


def _rms_norm(x, g):
    return x * lax.rsqrt(jnp.mean(x * x, axis=-1, keepdims=True) + RMS_EPS) * g


def _qk(q, k):
    return lax.dot_general(q, k, (((1,), (1,)), ((), ())), preferred_element_type=F32)


def _rope(slab, cos, sin):
    return slab * cos + pltpu.roll(slab, 64, 1) * sin


def _adaln_kernel(c_ref, w_ref, b_ref, o_ref):
    sc = _silu(c_ref[...])
    o_ref[0] = jnp.dot(sc, w_ref[0], preferred_element_type=F32, precision=lax.Precision.HIGHEST) + b_ref[0]


def _adaln(cc, ada_w, ada_b):
    depth, d, n = ada_w.shape
    tn = 1024
    out = pl.pallas_call(
        _adaln_kernel,
        grid=(depth, n // tn),
        in_specs=[pl.BlockSpec((MOD_ROWS, d), lambda l, j: (0, 0)),
                  pl.BlockSpec((1, d, tn), lambda l, j: (l, 0, j)),
                  pl.BlockSpec((1, 1, tn), lambda l, j: (l, 0, j))],
        out_specs=pl.BlockSpec((1, MOD_ROWS, tn), lambda l, j: (l, 0, j)),
        out_shape=jax.ShapeDtypeStruct((depth, MOD_ROWS, n), F32),
        compiler_params=_cparams(2),
        name="adaln",
    )(cc, ada_w, ada_b.reshape(depth, 1, n))
    return out.reshape(depth, MOD_ROWS, 6, 1, d)


def _qkv_kernel(*refs, plan, use_rope):
    if use_rope:
        mod_ref, x_ref, w_ref, cos_ref, sin_ref = refs[:5]
        out_refs = refs[5:]
        cos, sin = cos_ref[...], sin_ref[...]
    else:
        mod_ref, x_ref, w_ref = refs[:3]
        out_refs = refs[3:]
    sh, sc = mod_ref[0, 0], mod_ref[0, 1]
    a = (x_ref[0] * (1.0 + sc) + sh).astype(BF16)
    y = jnp.dot(a, w_ref[...], preferred_element_type=F32)
    for o_ref, (c0, width, rope) in zip(out_refs, plan):
        if rope:
            for j in range(width // LANES):
                slab = y[:, c0 + LANES * j:c0 + LANES * (j + 1)]
                o_ref[0, :, LANES * j:LANES * (j + 1)] = _rope(slab, cos, sin).astype(BF16)
        else:
            o_ref[0] = y[:, c0:c0 + width].astype(BF16)


def _qkv_proj(h, mods, w, plan, cos=None, sin=None, *, name):
    b, tt, d = h.shape
    n = w.shape[1]
    use_rope = cos is not None
    nt = tt // TILE
    in_specs = [pl.BlockSpec((1, 6, 1, d), lambda i, t: (jnp.where(t == 0, b, i), 0, 0, 0)),
                pl.BlockSpec((1, TILE, d), lambda i, t: (i, t, 0)),
                _const_spec((d, n))]
    args = [mods, h, w]
    if use_rope:
        in_specs += [pl.BlockSpec((TILE, LANES), lambda i, t: (t, 0))] * 2
        args += [cos, sin]
    return pl.pallas_call(
        functools.partial(_qkv_kernel, plan=plan, use_rope=use_rope),
        grid=(b, nt),
        in_specs=in_specs,
        out_specs=[pl.BlockSpec((1, TILE, wd), lambda i, t: (i, t, 0)) for _, wd, _ in plan],
        out_shape=[jax.ShapeDtypeStruct((b, tt, wd), BF16) for _, wd, _ in plan],
        compiler_params=_cparams(2),
        name=name,
    )(*args)


def _mla_proj_kernel(mod_ref, x_ref, w1_ref, qn_ref, kvn_ref, wqb_ref, wkvb_ref, cos_ref, sin_ref,
                     qr_ref, qu_ref, k_ref, v_ref):
    sh, sc = mod_ref[0, 0], mod_ref[0, 1]
    cos, sin = cos_ref[...], sin_ref[...]
    a = (x_ref[0] * (1.0 + sc) + sh).astype(BF16)
    y1 = jnp.dot(a, w1_ref[...], preferred_element_type=F32)
    cq = _rms_norm(y1[:, :MLA_QL], qn_ref[...]).astype(BF16)
    ckv = _rms_norm(y1[:, MLA_QL:MLA_QL + MLA_KVL], kvn_ref[...]).astype(BF16)
    pe = _rope(y1[:, MLA_QL + MLA_KVL:], cos, sin)
    q = jnp.dot(cq, wqb_ref[...], preferred_element_type=F32)
    kv = jnp.dot(ckv, wkvb_ref[...], preferred_element_type=F32)
    for hd in range(MLA_HEADS):
        sl = slice(LANES * hd, LANES * (hd + 1))
        qs = q[:, sl]
        qu_ref[0, :, sl] = qs.astype(BF16)
        qr_ref[0, :, sl] = _rope(qs, cos, sin).astype(BF16)
        k_ref[0, :, sl] = (kv[:, sl] + pe).astype(BF16)
    v_ref[0] = kv[:, MLA_HEADS * LANES:].astype(BF16)


def _mla_proj(h, mods, w1, qn, kvn, wqb, wkvb, cos, sin):
    b, tt, d = h.shape
    nt = tt // TILE
    hw = MLA_HEADS * LANES
    vw = MLA_HEADS * MLA_V
    row = lambda i, t: (i, t, 0)
    return pl.pallas_call(
        _mla_proj_kernel,
        grid=(b, nt),
        in_specs=[pl.BlockSpec((1, 6, 1, d), lambda i, t: (jnp.where(t == 0, b, i), 0, 0, 0)),
                  pl.BlockSpec((1, TILE, d), row),
                  _const_spec(w1.shape), _const_spec(qn.shape), _const_spec(kvn.shape),
                  _const_spec(wqb.shape), _const_spec(wkvb.shape),
                  pl.BlockSpec((TILE, LANES), lambda i, t: (t, 0)),
                  pl.BlockSpec((TILE, LANES), lambda i, t: (t, 0))],
        out_specs=[pl.BlockSpec((1, TILE, hw), row), pl.BlockSpec((1, TILE, hw), row),
                   pl.BlockSpec((1, TILE, hw), row), pl.BlockSpec((1, TILE, vw), row)],
        out_shape=[jax.ShapeDtypeStruct((b, tt, hw), BF16)] * 3 + [jax.ShapeDtypeStruct((b, tt, vw), BF16)],
        compiler_params=_cparams(2),
        name="mla_proj",
    )(mods, h, w1, qn, kvn, wqb, wkvb, cos, sin)


def _dense_attn_kernel(*refs, kind, ctx_len, tk, n_lat, lam_init):
    if kind == "mla":
        qr_ref, qu_ref, k_ref, vt_ref, o_ref, m_sc, l_sc, acc_sc = refs
    else:
        qr_ref, qu_ref, k_ref, vt_ref, lam_ref, sub_ref, o_ref, m_sc, l_sc, acc_sc = refs
    t = pl.program_id(2)
    qw = k_ref.shape[2]
    lane = lax.broadcasted_iota(jnp.int32, (1, qw), 1)
    sub_tiles = tk // TILE
    tq = qr_ref.shape[1]

    def queries(q_ref):
        q = q_ref[0]
        own = (lane < LANES) if kind == "mla" else ((lane // 32) % 2 == 0)
        zero = jnp.zeros_like(q)
        return jnp.concatenate([jnp.where(own, q, zero), jnp.where(own, zero, q)], axis=0)

    st = _qk(k_ref[0, 0:ctx_len, :], queries(qu_ref))
    m = jnp.max(st, axis=0, keepdims=True)
    p = jnp.exp2(st - m)
    m_sc[...] = m
    l_sc[...] = jnp.sum(p, axis=0, keepdims=True)
    acc_sc[...] = jnp.dot(vt_ref[0, 0, 0], p.astype(BF16), preferred_element_type=F32)

    @pl.when(t > 0)
    def _latent():
        q_rot = queries(qr_ref)

        def body(i, carry):
            m0, l0, acc0 = carry
            start = pl.multiple_of(ctx_len + i * tk, TILE)
            st = _qk(k_ref[0, pl.ds(start, tk), :], q_rot)
            m1 = jnp.maximum(m0, jnp.max(st, axis=0, keepdims=True))
            alpha = jnp.exp2(m0 - m1)
            p = jnp.exp2(st - m1)
            l1 = alpha * l0 + jnp.sum(p, axis=0, keepdims=True)
            pb = p.astype(BF16)
            pv = None
            for j in range(sub_tiles):
                d = jnp.dot(vt_ref[0, 0, 1 + i * sub_tiles + j], pb[TILE * j:TILE * (j + 1)],
                            preferred_element_type=F32)
                pv = d if pv is None else pv + d
            return m1, l1, alpha * acc0 + pv

        m2, l2, acc2 = lax.fori_loop(0, n_lat, body, (m_sc[...], l_sc[...], acc_sc[...]), unroll=True)
        m_sc[...] = m2
        l_sc[...] = l2
        acc_sc[...] = acc2

    o = acc_sc[...] / l_sc[...]
    o0, o1 = o[:, :tq], o[:, tq:]
    if kind == "mla":
        o = jnp.concatenate([o0[:64], o1[64:]], axis=0)
    else:
        lf = lam_ref[...]
        lam = (jnp.exp(jnp.sum(lf[0:1] * lf[1:2], axis=1, keepdims=True))
               - jnp.exp(jnp.sum(lf[2:3] * lf[3:4], axis=1, keepdims=True)) + lam_init)
        o = o0 - lam * o1
        o = o * lax.rsqrt(jnp.mean(o * o, axis=0, keepdims=True) + RMS_EPS) * sub_ref[...] * (1.0 - lam_init)
    o_ref[0] = o.T.astype(BF16)


def _dense_attn(qr, qu, k, v, *, kind, ctx_len, lam=None, subln=None, lam_init=0.0):
    b, tt, _ = v.shape
    groups = v.shape[2] // LANES
    qw = qr.shape[2] // groups
    nt = tt // TILE
    tk = min(DENSE_TK, tt - ctx_len)
    n_lat = (tt - ctx_len) // tk
    vt = v.reshape(b, nt, TILE, groups, LANES).transpose(0, 3, 1, 4, 2)
    in_specs = [pl.BlockSpec((1, TILE, qw), lambda i, g, t: (i, t, g)),
                pl.BlockSpec((1, TILE, qw), lambda i, g, t: (i, t, g)),
                pl.BlockSpec((1, tt, qw), lambda i, g, t: (i, 0, g)),
                pl.BlockSpec((1, 1, nt, LANES, TILE), lambda i, g, t: (i, g, 0, 0, 0))]
    args = [qr, qu, k, vt]
    if kind == "diff":
        in_specs += [_const_spec(lam.shape), _const_spec(subln.shape)]
        args += [lam, subln]
    return pl.pallas_call(
        functools.partial(_dense_attn_kernel, kind=kind, ctx_len=ctx_len, tk=tk, n_lat=n_lat, lam_init=lam_init),
        grid=(b, groups, nt),
        in_specs=in_specs,
        out_specs=pl.BlockSpec((1, TILE, LANES), lambda i, g, t: (i, t, g)),
        out_shape=jax.ShapeDtypeStruct((b, tt, groups * LANES), BF16),
        scratch_shapes=[pltpu.VMEM((1, 2 * TILE), F32), pltpu.VMEM((1, 2 * TILE), F32),
                        pltpu.VMEM((LANES, 2 * TILE), F32)],
        compiler_params=_cparams(3),
        name=kind + "_attn",
    )(*args)


def _softmax_pv(segments, sink=None):
    m = None
    for s, _ in segments:
        ms = jnp.max(s, axis=1, keepdims=True)
        m = ms if m is None else jnp.maximum(m, ms)
    if sink is not None:
        m = jnp.maximum(m, sink)
    l = jnp.exp2(sink - m) if sink is not None else 0.0
    acc = None
    for s, v in segments:
        p = jnp.exp2(s - m)
        l = l + jnp.sum(p, axis=1, keepdims=True)
        pv = jnp.dot(p.astype(BF16), v, preferred_element_type=F32)
        acc = pv if acc is None else acc + pv
    return acc / l


def _na_attn_kernel(q_ref, k_ref, v_ref, bias_ref, o_ref, *, ctx_len, n_rows):
    t = pl.program_id(2)
    n_ctx_tiles = ctx_len // NA_TQ
    lane = lax.broadcasted_iota(jnp.int32, (1, LANES), 1)
    nk = NA_KROWS * GRID_W

    def head_pair(i):
        sl = slice(LANES * i, LANES * (i + 1))
        q = q_ref[0, :, sl]
        zero = jnp.zeros_like(q)
        return sl, jnp.concatenate([jnp.where(lane < 64, q, zero), jnp.where(lane < 64, zero, q)], axis=0)

    def store(sl, o):
        o_ref[0, :, sl] = jnp.where(lane < 64, o[:NA_TQ], o[NA_TQ:]).astype(BF16)

    @pl.when(t < n_ctx_tiles)
    def _ctx():
        for i in range(q_ref.shape[2] // LANES):
            sl, qx = head_pair(i)
            kc, vc = k_ref[0, 0:ctx_len, sl], v_ref[0, 0:ctx_len, sl]
            store(sl, _softmax_pv([(_qk(qx, kc), vc)]))

    @pl.when(t >= n_ctx_tiles)
    def _lat():
        j = t - n_ctx_tiles
        row0 = jnp.clip(2 * j - NA_ROWS // 2, 0, n_rows - NA_KROWS)
        start = pl.multiple_of(ctx_len + row0 * GRID_W, GRID_W)
        for i in range(q_ref.shape[2] // LANES):
            sl, qx = head_pair(i)
            kc, vc = k_ref[0, 0:ctx_len, sl], v_ref[0, 0:ctx_len, sl]
            kb, vb = k_ref[0, pl.ds(start, nk), sl], v_ref[0, pl.ds(start, nk), sl]
            bias = bias_ref[0, 2 * i:2 * i + 2].reshape(2 * NA_TQ, nk)
            store(sl, _softmax_pv([(_qk(qx, kc), vc), (_qk(qx, kb) + bias, vb)]))


def _na_tile_class(j, n_tiles):
    return jnp.where(j < 2, j, jnp.where(j >= n_tiles - 2, j - (n_tiles - 2) + 3, 2))


def _na_attn(q, k, v, bias, *, ctx_len):
    b, tt, w = q.shape
    groups = w // LANES
    n_rows = (tt - ctx_len) // GRID_W
    n_ctx_tiles = ctx_len // NA_TQ
    n_lat_tiles = n_rows // 2
    nk = NA_KROWS * GRID_W
    gw = NA_GROUPS * LANES

    def bias_map(i, g, t):
        return (_na_tile_class(jnp.maximum(t - n_ctx_tiles, 0), n_lat_tiles), g, 0, 0)

    return pl.pallas_call(
        functools.partial(_na_attn_kernel, ctx_len=ctx_len, n_rows=n_rows),
        grid=(b, groups // NA_GROUPS, tt // NA_TQ),
        in_specs=[pl.BlockSpec((1, NA_TQ, gw), lambda i, g, t: (i, t, g)),
                  pl.BlockSpec((1, tt, gw), lambda i, g, t: (i, 0, g)),
                  pl.BlockSpec((1, tt, gw), lambda i, g, t: (i, 0, g)),
                  pl.BlockSpec((1, 2 * NA_GROUPS, NA_TQ, nk), bias_map)],
        out_specs=pl.BlockSpec((1, NA_TQ, gw), lambda i, g, t: (i, t, g)),
        out_shape=jax.ShapeDtypeStruct((b, tt, w), BF16),
        compiler_params=_cparams(3),
        name="na_attn",
    )(q, k, v, bias)


def _na_bias_table(rpb, n_rows):
    heads, n_ri, n_ci = rpb.shape
    n_tiles = n_rows // 2
    w = jnp.pad(rpb.astype(F32) * LOG2E, ((0, 0), (0, 0), (0, LANES - n_ci)))
    r = jnp.broadcast_to(w[:, :, None, :], (heads, n_ri, GRID_W, LANES)).reshape(heads, n_ri, GRID_W * LANES)
    r = r[:, :, :GRID_W * (LANES - 1)].reshape(heads, n_ri, GRID_W, LANES - 1)
    toep = r[..., NA_COLS - 1:NA_COLS - 1 + GRID_W]
    c = np.arange(GRID_W)
    cs = np.clip(c - NA_COLS // 2, 0, GRID_W - NA_COLS)
    col_ok = (c[None, :] >= cs[:, None]) & (c[None, :] < cs[:, None] + NA_COLS)
    toep = jnp.where(jnp.asarray(col_ok), toep, NEG_INF)
    masked = jnp.full((heads, GRID_W, GRID_W), NEG_INF, F32)
    classes = []
    for rep in (0, 1, 2, n_tiles - 2, n_tiles - 1):
        row0 = int(np.clip(2 * rep - NA_ROWS // 2, 0, n_rows - NA_KROWS))
        q_rows = []
        for a in range(2):
            qr = 2 * rep + a
            r0 = int(np.clip(qr - NA_ROWS // 2, 0, n_rows - NA_ROWS))
            blocks = [toep[:, kr - qr + NA_ROWS - 1] if r0 <= kr < r0 + NA_ROWS else masked
                      for kr in range(row0, row0 + NA_KROWS)]
            q_rows.append(jnp.concatenate(blocks, axis=-1))
        classes.append(jnp.concatenate(q_rows, axis=1))
    return jnp.stack(classes)


def _swa_attn_kernel(qr_ref, qu_ref, k_ref, v_ref, sink_ref, o_ref, *, ctx_len, seq):
    t = pl.program_id(2)
    band = TILE + 2 * SWA_WINDOW
    lane = lax.broadcasted_iota(jnp.int32, (1, LANES), 1)
    kc, vc = k_ref[0, 0:ctx_len, :], v_ref[0, 0:ctx_len, :]
    n_slabs = qr_ref.shape[2] // LANES

    def heads(q_ref):
        out = []
        for i in range(n_slabs):
            slab = q_ref[0, :, LANES * i:LANES * (i + 1)]
            for s in range(2):
                out.append(jnp.where(((lane // 32) % 2) == s, slab, jnp.zeros_like(slab)))
        return out

    def sink(idx):
        return sink_ref[0, idx:idx + 1, 0:1]

    def store(outs):
        for i in range(n_slabs):
            o_ref[0, :, LANES * i:LANES * (i + 1)] = jnp.where(lane < 64, outs[2 * i], outs[2 * i + 1]).astype(BF16)

    @pl.when(t == 0)
    def _ctx():
        store([_softmax_pv([(_qk(q, kc), vc)], sink(n)) for n, q in enumerate(heads(qu_ref))])

    @pl.when(t > 0)
    def _lat():
        p0 = (t - 1) * TILE
        rel = jnp.clip(p0 - SWA_WINDOW, 0, seq - band)
        start = pl.multiple_of(ctx_len + rel, SWA_WINDOW)
        kb = k_ref[0, pl.ds(start, band), :]
        vb = v_ref[0, pl.ds(start, band), :]
        qpos = p0 + lax.broadcasted_iota(jnp.int32, (TILE, band), 0)
        kpos = rel + lax.broadcasted_iota(jnp.int32, (TILE, band), 1)
        valid = jnp.abs(qpos - kpos) <= SWA_WINDOW
        outs = []
        for n, (qu, qr) in enumerate(zip(heads(qu_ref), heads(qr_ref))):
            s_l = jnp.where(valid, _qk(qr, kb), NEG_INF)
            outs.append(_softmax_pv([(_qk(qu, kc), vc), (s_l, vb)], sink(n)))
        store(outs)


def _swa_attn(qr, qu, k, v, sinks, *, ctx_len):
    b, tt, w = qr.shape
    groups = k.shape[2] // LANES
    qw = w // groups
    return pl.pallas_call(
        functools.partial(_swa_attn_kernel, ctx_len=ctx_len, seq=tt - ctx_len),
        grid=(b, groups, tt // TILE),
        in_specs=[pl.BlockSpec((1, TILE, qw), lambda i, g, t: (i, t, g)),
                  pl.BlockSpec((1, TILE, qw), lambda i, g, t: (i, t, g)),
                  pl.BlockSpec((1, tt, LANES), lambda i, g, t: (i, 0, g)),
                  pl.BlockSpec((1, tt, LANES), lambda i, g, t: (i, 0, g)),
                  pl.BlockSpec((1, sinks.shape[1], LANES), lambda i, g, t: (g, 0, 0))],
        out_specs=pl.BlockSpec((1, TILE, qw), lambda i, g, t: (i, t, g)),
        out_shape=jax.ShapeDtypeStruct((b, tt, w), BF16),
        compiler_params=_cparams(3),
        name="swa_attn",
    )(qr, qu, k, v, sinks)


def _out_kernel(mod_ref, o_ref, w_ref, h_ref, g_ref, b_ref, out_ref, *, alpha):
    y = jnp.dot(o_ref[0], w_ref[...], preferred_element_type=F32)
    out_ref[0] = _layer_norm(alpha * h_ref[0] + mod_ref[0, 2] * y, g_ref[...], b_ref[...])


def _out_proj(o, w_o, h, mods, g, bb, *, alpha):
    b, tt, d = h.shape
    row = lambda i, t: (i, t, 0)
    return pl.pallas_call(
        functools.partial(_out_kernel, alpha=alpha),
        grid=(b, tt // TILE),
        in_specs=[pl.BlockSpec((1, 6, 1, d), lambda i, t: (jnp.where(t == 0, b, i), 0, 0, 0)),
                  pl.BlockSpec((1, TILE, o.shape[2]), row),
                  _const_spec(w_o.shape),
                  pl.BlockSpec((1, TILE, d), row),
                  _const_spec(g.shape), _const_spec(bb.shape)],
        out_specs=pl.BlockSpec((1, TILE, d), row),
        out_shape=jax.ShapeDtypeStruct((b, tt, d), F32),
        compiler_params=_cparams(2),
        name="out_proj_ln",
    )(mods, o, w_o, h, g, bb)


def _ffn_kernel(mod_ref, h_ref, hp_ref, hn_ref, win_ref, cw_ref, cb_ref, wout_ref, g_ref, b_ref, out_ref,
                *, alpha, n_chunks, first_lat, last):
    t = pl.program_id(1)
    sh, sc, gate = mod_ref[0, 3], mod_ref[0, 4], mod_ref[0, 5]
    h = h_ref[0]
    has_prev = jnp.where((t == 0) | (t == first_lat), 0.0, 1.0)
    has_next = jnp.where((t == first_lat - 1) | (t == last), 0.0, 1.0)
    mod = lambda x: x * (1.0 + sc) + sh
    xe = jnp.concatenate([mod(hp_ref[0]) * has_prev, mod(h), mod(hn_ref[0]) * has_next], axis=0).astype(BF16)
    rows = h.shape[0]

    def up(c):
        return jnp.dot(xe, win_ref[c], preferred_element_type=F32)

    def down(c, u, acc):
        cw = cw_ref[c]
        conv = (cw[0:1] * pltpu.roll(u, 1, 0)[8:8 + rows] + cw[1:2] * u[8:8 + rows]
                + cw[2:3] * pltpu.roll(u, rows + 15, 0)[8:8 + rows] + cb_ref[c])
        act = (_silu(conv[:, FFN_CHUNK:]) * conv[:, :FFN_CHUNK]).astype(BF16)
        return acc + jnp.dot(act, wout_ref[c], preferred_element_type=F32)

    def chunk(c, carry):
        acc, u = carry
        u_next = up(c + 1)
        return down(c, u, acc), u_next

    acc, u_last = lax.fori_loop(0, n_chunks - 1, chunk, (jnp.zeros(h.shape, F32), up(0)), unroll=True)
    f = down(n_chunks - 1, u_last, acc)
    out_ref[0] = _layer_norm(alpha * h + gate * f, g_ref[...], b_ref[...])


def _conv_ffn(h, mods, win, cw, cb, wout, g, bb, *, alpha, ctx_len):
    b, tt, d = h.shape
    nt = tt // TILE
    halo = 8
    per = TILE // halo
    row = lambda i, t: (i, t, 0)
    return pl.pallas_call(
        functools.partial(_ffn_kernel, alpha=alpha, n_chunks=win.shape[0], first_lat=ctx_len // TILE, last=nt - 1),
        grid=(b, nt),
        in_specs=[pl.BlockSpec((1, 6, 1, d), lambda i, t: (jnp.where(t == 0, b, i), 0, 0, 0)),
                  pl.BlockSpec((1, TILE, d), row),
                  pl.BlockSpec((1, halo, d), lambda i, t: (i, jnp.maximum(t * per - 1, 0), 0)),
                  pl.BlockSpec((1, halo, d), lambda i, t: (i, jnp.minimum((t + 1) * per, tt // halo - 1), 0)),
                  _const_spec(win.shape), _const_spec(cw.shape), _const_spec(cb.shape), _const_spec(wout.shape),
                  _const_spec(g.shape), _const_spec(bb.shape)],
        out_specs=pl.BlockSpec((1, TILE, d), row),
        out_shape=jax.ShapeDtypeStruct((b, tt, d), F32),
        compiler_params=_cparams(2),
        name="conv_ffn_ln",
    )(mods, h, h, h, win, cw, cb, wout, g, bb)


def _pair_perm():
    return np.concatenate([np.arange(0, 32), np.arange(64, 96), np.arange(32, 64), np.arange(96, 128)])


def _take_cols(w, src):
    src = np.asarray(src)
    cols = jnp.take(w, jnp.asarray(np.maximum(src, 0)), axis=1)
    return jnp.where(jnp.asarray(src >= 0)[None, :], cols, 0.0)


def _axial_rope(n_tokens, rot_dim):
    t = jnp.arange(n_tokens)
    row = (t // GRID_W).astype(F32)
    col = (t % GRID_W).astype(F32)
    n_freq = rot_dim // 4
    inv_freq = ROPE_BASE ** (-jnp.arange(n_freq, dtype=F32) / n_freq)
    ang = jnp.concatenate([row[:, None] * inv_freq, col[:, None] * inv_freq], -1)
    return jnp.cos(ang), jnp.sin(ang)


def _rope_tables(seq, ctx_len, kind):
    if kind == "mla":
        cos, sin = _axial_rope(seq, MLA_ROPE)
        c = jnp.ones((seq, LANES), F32).at[:, 48:64].set(cos).at[:, 112:128].set(cos)
        s = jnp.zeros((seq, LANES), F32).at[:, 48:64].set(-sin).at[:, 112:128].set(sin)
    else:
        cos, sin = _axial_rope(seq, 64)
        c = jnp.concatenate([cos] * 4, axis=1)
        s = jnp.concatenate([-sin, -sin, sin, sin], axis=1)
    c = jnp.concatenate([jnp.ones((ctx_len, LANES), F32), c], axis=0)
    s = jnp.concatenate([jnp.zeros((ctx_len, LANES), F32), s], axis=0)
    return c, s


def _mla_head_lanes():
    nope = np.concatenate([np.arange(0, 48), np.arange(64, 80)])
    rope_a, rope_b = np.arange(48, 64), np.arange(112, 128)
    return nope, rope_a, rope_b


def _mla_weights(w_in, w_qb, w_kvb):
    nope, rope_a, rope_b = _mla_head_lanes()
    dq = MLA_NOPE + MLA_ROPE
    src_pe = np.full(LANES, -1)
    src_pe[rope_a] = MLA_QL + MLA_KVL + np.arange(16)
    src_pe[rope_b] = MLA_QL + MLA_KVL + 16 + np.arange(16)
    w1 = jnp.concatenate([w_in[:, :MLA_QL + MLA_KVL], _take_cols(w_in, src_pe)], axis=1)
    src_q = np.full(MLA_HEADS * LANES, -1)
    src_k = np.full(MLA_HEADS * LANES, -1)
    src_v = np.zeros(MLA_HEADS * MLA_V, np.int64)
    for hd in range(MLA_HEADS):
        src_q[hd * LANES + nope] = hd * dq + np.arange(MLA_NOPE)
        src_q[hd * LANES + rope_a] = hd * dq + MLA_NOPE + np.arange(16)
        src_q[hd * LANES + rope_b] = hd * dq + MLA_NOPE + 16 + np.arange(16)
        src_k[hd * LANES + nope] = hd * (MLA_NOPE + MLA_V) + np.arange(MLA_NOPE)
        src_v[hd * MLA_V:(hd + 1) * MLA_V] = hd * (MLA_NOPE + MLA_V) + MLA_NOPE + np.arange(MLA_V)
    wqb = _take_cols(w_qb, src_q) * (dq ** -0.5 * LOG2E)
    wkvb = jnp.concatenate([_take_cols(w_kvb, src_k), _take_cols(w_kvb, src_v)], axis=1)
    return w1.astype(BF16), wqb.astype(BF16), wkvb.astype(BF16)


def _swa_layout():
    rep = SWA_HEADS // SWA_KV
    head_a = [2 * (j // rep) * rep + j % rep for j in range(SWA_HEADS // 2)]
    head_b = [(2 * (j // rep) + 1) * rep + j % rep for j in range(SWA_HEADS // 2)]
    return head_a, head_b


def kernel(x, c, ctx, c_ctx, ada_w, ada_b, ln1_g, ln1_b, ln2_g, ln2_b, ffn_w_in, ffn_conv_w, ffn_conv_b, ffn_w_out, mla_w_in, mla_q_norm, mla_kv_norm, mla_w_qb, mla_w_kvb, mla_w_o, na_w_qkv, na_rpb, na_w_o, diff_w_qkv, diff_lambda, diff_subln, diff_w_o, swa_w_qkv, swa_sinks, swa_w_o):
    b, seq, d = x.shape
    ctx_len = ctx.shape[1]
    depth = ada_w.shape[0]
    assert ctx_len == TILE and seq % min(DENSE_TK, seq) == 0 and seq % (2 * GRID_W) == 0 and b + 1 <= MOD_ROWS
    alpha = (2.0 * depth) ** 0.25
    n_rows = seq // GRID_W
    perm = _pair_perm()

    cc = jnp.concatenate([c, c_ctx[None, :], jnp.zeros((MOD_ROWS - b - 1, d), F32)], axis=0)
    mods_all = _adaln(cc, ada_w, ada_b)
    h = jnp.concatenate([ctx, x], axis=1)

    f_hidden = ffn_w_out.shape[1]
    n_chunks = f_hidden // FFN_CHUNK

    for i in range(depth):
        mods = mods_all[i]
        kind, slot = i % 4, i // 4
        if kind == 0:
            w1, wqb, wkvb = _mla_weights(mla_w_in[slot], mla_w_qb[slot], mla_w_kvb[slot])
            cos, sin = _rope_tables(seq, ctx_len, "mla")
            qr, qu, k, v = _mla_proj(h, mods, w1, mla_q_norm[slot][None, :], mla_kv_norm[slot][None, :],
                                     wqb, wkvb, cos, sin)
            o = _dense_attn(qr, qu, k, v, kind="mla", ctx_len=ctx_len)
            w_o = mla_w_o[slot]
        elif kind == 1:
            nq = NA_HEADS * NA_DIM
            w = na_w_qkv[slot]
            w = jnp.concatenate([w[:, :nq] * (NA_DIM ** -0.5 * LOG2E), w[:, nq:]], axis=1).astype(BF16)
            q, k, v = _qkv_proj(h, mods, w, ((0, nq, False), (nq, nq, False), (2 * nq, nq, False)), name="na_proj")
            o = _na_attn(q, k, v, _na_bias_table(na_rpb[slot], n_rows), ctx_len=ctx_len)
            w_o = na_w_o[slot]
        elif kind == 2:
            nq = 2 * DIFF_HEADS * DIFF_DIM
            src = np.concatenate([g * LANES + perm for g in range(2 * nq // LANES)] + [np.arange(2 * nq, 3 * nq)])
            w = _take_cols(diff_w_qkv[slot], src)
            w = jnp.concatenate([w[:, :nq] * (DIFF_DIM ** -0.5 * LOG2E), w[:, nq:]], axis=1).astype(BF16)
            cos, sin = _rope_tables(seq, ctx_len, "pair")
            qr, qu, k, v = _qkv_proj(h, mods, w, ((0, nq, True), (0, nq, False), (nq, nq, True), (2 * nq, nq, False)),
                                     cos, sin, name="diff_proj")
            lam_init = 0.8 - 0.6 * math.exp(-0.3 * i)
            o = _dense_attn(qr, qu, k, v, kind="diff", ctx_len=ctx_len, lam=diff_lambda[slot],
                            subln=diff_subln[slot][:, None], lam_init=lam_init)
            w_o = diff_w_o[slot]
        else:
            nq, nkv = SWA_HEADS * SWA_DIM, SWA_KV * SWA_DIM
            head_a, head_b = _swa_layout()
            pair_src = lambda ha, hb, base: base + np.concatenate([ha * 64 + np.arange(64), hb * 64 + np.arange(64)])[perm]
            src = np.concatenate([pair_src(ha, hb, 0) for ha, hb in zip(head_a, head_b)]
                                 + [pair_src(2 * g, 2 * g + 1, nq) for g in range(SWA_KV // 2)]
                                 + [np.arange(nq + nkv, nq + 2 * nkv)])
            w = _take_cols(swa_w_qkv[slot], src)
            w = jnp.concatenate([w[:, :nq] * (SWA_DIM ** -0.5 * LOG2E), w[:, nq:]], axis=1).astype(BF16)
            cos, sin = _rope_tables(seq, ctx_len, "pair")
            qr, qu, k, v = _qkv_proj(h, mods, w, ((0, nq, True), (0, nq, False), (nq, nkv, True), (nq + nkv, nkv, False)),
                                     cos, sin, name="swa_proj")
            order = np.array([hd for pair in zip(head_a, head_b) for hd in pair])
            per_group = len(order) // (SWA_KV // 2)
            sinks = jnp.broadcast_to((swa_sinks[slot].astype(F32) * LOG2E)[order].reshape(-1, per_group, 1),
                                     (SWA_KV // 2, per_group, LANES))
            o = _swa_attn(qr, qu, k, v, sinks, ctx_len=ctx_len)
            rows = np.concatenate([hd * 64 + np.arange(64) for hd in order])
            w_o = jnp.take(swa_w_o[slot], jnp.asarray(rows), axis=0)
        h = _out_proj(o, w_o.astype(BF16), h, mods, ln1_g[i][None, :], ln1_b[i][None, :], alpha=alpha)

        win = ffn_w_in[i].reshape(d, 2, n_chunks, FFN_CHUNK).transpose(2, 0, 1, 3).reshape(n_chunks, d, 2 * FFN_CHUNK)
        cw = ffn_conv_w[i].reshape(3, 2, n_chunks, FFN_CHUNK).transpose(2, 0, 1, 3).reshape(n_chunks, 3, 2 * FFN_CHUNK)
        cb = ffn_conv_b[i].reshape(2, n_chunks, FFN_CHUNK).transpose(1, 0, 2).reshape(n_chunks, 1, 2 * FFN_CHUNK)
        wout = ffn_w_out[i].reshape(n_chunks, FFN_CHUNK, d)
        h = _conv_ffn(h, mods, win.astype(BF16), cw, cb, wout.astype(BF16), ln2_g[i][None, :], ln2_b[i][None, :],
                      alpha=alpha, ctx_len=ctx_len)
    return h[:, ctx_len:, :]
```
